```python
import math
import functools
import jax
import jax.numpy as jnp
from jax import lax
import numpy as np

D_MODEL = 4096
BATCH = 4
SEQ = 2048
DEPTH = 4
DEC_BATCH = 8
DEC_SEQ = 1
PAST_LEN = 8192
PAGE_SIZE = 128

F32 = jnp.float32
NORM_EPS = 1e-6
ROPE_THETA = 10000.0
D_FF = ((8 * D_MODEL // 3 + 255) // 256) * 256

C_A = D_MODEL // 2
HD_A = 64
H_A = C_A // HD_A
LORA_W = max(32, int(round(1.8 * C_A ** 0.5 / 32)) * 32)
LORA_A = max(32, int(round(1.8 * C_A ** 0.5 / 32)) * 32)
LORA_G = max(32, int(round(0.6 * C_A ** 0.8 / 32)) * 32)
LN_X_EPS = 64e-5
N_A_IN = 3 * C_A + LORA_W + LORA_A + LORA_G

C_B = D_MODEL // 2
VD_B = 128
DH_B = VD_B // 2
H_B = C_B // VD_B
C_QK = H_B * 2 * DH_B
Q_BLOCK = 128

C_C = D_MODEL // 2
P_C = 64
H_C = C_C // P_C
G_C = 8
E_C = H_C // G_C
N_C = 128
CONV_W = 4
CONV_DIM = C_C + 2 * G_C * N_C
SSD_CHUNK = 128
MB_NORM_EPS = 1e-5

N_GATES = 3
N_IN = N_A_IN + 2 * C_QK + C_B + C_C + CONV_DIM + H_C + N_GATES * D_MODEL

kernel_name = 'hybrid_rwkv7_diffattn_mamba2_macaron_step'


def rms_norm(x, g, eps=NORM_EPS):
    xf = x.astype(F32)
    y = xf * lax.rsqrt(jnp.mean(xf * xf, axis=-1, keepdims=True) + eps)
    return (y * g.astype(F32)).astype(x.dtype)


def swiglu(x, w_gate, w_up, w_down):
    return (jax.nn.silu(x @ w_gate) * (x @ w_up)) @ w_down


def split_cols(u, sizes):
    offs = [int(s) for s in np.cumsum(sizes)[:-1]]
    return jnp.split(u, offs, axis=-1)


def rope(t, pos):
    half = t.shape[-1] // 2
    inv = ROPE_THETA ** (-jnp.arange(half, dtype=F32) / half)
    ang = pos.astype(F32)[:, None] * inv[None, :]
    cos = jnp.cos(ang)[None, :, None, None, :]
    sin = jnp.sin(ang)[None, :, None, None, :]
    t1 = t[..., :half].astype(F32)
    t2 = t[..., half:].astype(F32)
    return jnp.concatenate([t1 * cos - t2 * sin, t2 * cos + t1 * sin], axis=-1).astype(t.dtype)


def rwkv7_mixer(ua, shift0, wkv0, p):
    b, L, _ = ua.shape
    prev = jnp.concatenate([shift0[:, None, :].astype(ua.dtype), ua[:, :-1]], axis=1)
    us = (ua + p['rw_mu'] * (prev - ua)).astype(F32)
    r, k, v, wd, ad, gd = split_cols(us, [C_A, C_A, C_A, LORA_W, LORA_A, LORA_G])
    w = -jax.nn.softplus(-(p['rw_w0'] + jnp.tanh(wd) @ p['rw_w_up'])) - 0.5
    a = jax.nn.sigmoid(p['rw_a0'] + ad @ p['rw_a_up'])
    g = jax.nn.sigmoid(gd) @ p['rw_g_up']

    def heads(t):
        return t.reshape(b, L, H_A, HD_A)

    kk = heads(k * p['rw_k_k'])
    kk = kk / jnp.maximum(jnp.sqrt(jnp.sum(kk * kk, axis=-1, keepdims=True)), 1e-12)
    k = k * (1.0 + (a - 1.0) * p['rw_k_a'])
    r, k, v, a = heads(r), heads(k), heads(v), heads(a)
    decay = heads(jnp.exp(-jnp.exp(w)))

    def step(S, inp):
        r_t, d_t, k_t, v_t, kk_t, a_t = inp
        s_kk = jnp.einsum('bhvk,bhk->bhv', S, -kk_t)
        S = (S * d_t[:, :, None, :]
             + s_kk[..., None] * (kk_t * a_t)[:, :, None, :]
             + v_t[..., None] * k_t[:, :, None, :])
        return S, jnp.einsum('bhvk,bhk->bhv', S, r_t)

    xs = tuple(jnp.moveaxis(t, 1, 0) for t in (r, decay, k, v, kk, a))
    wkv1, y = lax.scan(step, wkv0.astype(F32), xs)
    y = jnp.moveaxis(y, 0, 1)
    yc = y - jnp.mean(y, axis=-1, keepdims=True)
    y = yc * lax.rsqrt(jnp.mean(yc * yc, axis=-1, keepdims=True) + LN_X_EPS)
    y = y.reshape(b, L, C_A) * p['rw_ln_w'] + p['rw_ln_b']
    bonus = jnp.sum(r * k * p['rw_r_k'], axis=-1, keepdims=True) * v
    out = (y + bonus.reshape(b, L, C_A)) * g
    return out.astype(ua.dtype), ua[:, -1], wkv1


def diff_qkv(uq, uk, uv, p, pos):
    b, L, _ = uq.shape
    q = rope(rms_norm(uq.reshape(b, L, H_B, 2, DH_B), p['da_q_norm']), pos)
    k = rope(rms_norm(uk.reshape(b, L, H_B, 2, DH_B), p['da_k_norm']), pos)
    v = uv.reshape(b, L, H_B, VD_B)
    return q, k, v


def diff_lambda(lam_p, lam_init):
    lp = lam_p.astype(F32)
    return jnp.exp(jnp.sum(lp[0] * lp[1])) - jnp.exp(jnp.sum(lp[2] * lp[3])) + lam_init


def diff_attend(q, k, v, lam, qpos, kpos):
    s = jnp.einsum('bqhcd,bkhcd->bhcqk', q, k, preferred_element_type=F32) * (DH_B ** -0.5)
    s = jnp.where(kpos[None, :] <= qpos[:, None], s, -jnp.inf)
    pr = jax.nn.softmax(s, axis=-1)
    a = pr[:, :, 0] - lam * pr[:, :, 1]
    return jnp.einsum('bhqk,bkhv->bqhv', a.astype(v.dtype), v)


def attn_prompt(q, k, v, lam, pos):
    b, L = q.shape[:2]
    nb = L // Q_BLOCK
    qb = jnp.moveaxis(q.reshape(b, nb, Q_BLOCK, H_B, 2, DH_B), 1, 0)
    pb = pos.reshape(nb, Q_BLOCK)
    o = lax.map(lambda a: diff_attend(a[0], k, v, lam, a[1], pos), (qb, pb))
    return jnp.moveaxis(o, 0, 1).reshape(b, L, H_B, VD_B)


def attn_sample(q, k, v, lam, pos, k_cache, v_cache, page_table):
    b, L = q.shape[:2]
    n_past = page_table.shape[1] * k_cache.shape[1]
    k_past = k_cache[page_table].reshape(b, n_past, H_B, 2, DH_B).astype(k.dtype)
    v_past = v_cache[page_table].reshape(b, n_past, H_B, VD_B).astype(v.dtype)
    k_all = jnp.concatenate([k_past, k], axis=1)
    v_all = jnp.concatenate([v_past, v], axis=1)
    return diff_attend(q, k_all, v_all, lam, pos, jnp.arange(n_past + L))


def diff_output(o, subln, lam_init):
    b, L = o.shape[:2]
    return (rms_norm(o, subln) * (1.0 - lam_init)).reshape(b, L, C_B)


def segsum(x):
    T = x.shape[-1]
    xe = jnp.broadcast_to(x[..., :, None], x.shape + (T,))
    xe = jnp.where(jnp.tril(jnp.ones((T, T), bool), -1), xe, 0.0)
    cs = jnp.cumsum(xe, axis=-2)
    return jnp.where(jnp.tril(jnp.ones((T, T), bool)), cs, -jnp.inf)


def ssd_scan(X, A, B, C, h0):
    b, L = X.shape[:2]
    T = SSD_CHUNK if L % SSD_CHUNK == 0 else L
    nc = L // T
    X = X.reshape(b, nc, T, G_C, E_C, P_C)
    B = B.reshape(b, nc, T, G_C, N_C)
    C = C.reshape(b, nc, T, G_C, N_C)
    A = jnp.moveaxis(A.reshape(b, nc, T, G_C, E_C), (1, 2), (3, 4))
    A_cum = jnp.cumsum(A, axis=-1)
    Lmat = jnp.exp(segsum(A))
    CB = jnp.einsum('bclgn,bcsgn->bgcls', C, B)
    Y_diag = jnp.einsum('bgecls,bcsgep->bclgep', CB[:, :, None] * Lmat, X)
    decay_states = jnp.exp(A_cum[..., -1:] - A_cum)
    states = jnp.einsum('bclgn,bgecl,bclgep->bcgepn', B, decay_states, X)
    states = jnp.concatenate([h0[:, None], states], axis=1)
    chunk_tot = jnp.pad(A_cum[..., -1], ((0, 0), (0, 0), (0, 0), (1, 0)))
    decay_chunk = jnp.exp(segsum(chunk_tot))
    new_states = jnp.einsum('bgezc,bcgepn->bzgepn', decay_chunk, states)
    states, h_fin = new_states[:, :-1], new_states[:, -1]
    Y_off = jnp.einsum('bclgn,bcgepn,bgecl->bclgep', C, states, jnp.exp(A_cum))
    return (Y_diag + Y_off).reshape(b, L, G_C, E_C, P_C), h_fin


def mamba2_mixer(uz, uxbc, udt, conv0, ssm0, p):
    b, L, _ = uxbc.shape
    xpad = jnp.concatenate([conv0.astype(uxbc.dtype), uxbc], axis=1)
    conv = p['mb_conv_b'] + sum(xpad[:, j:j + L] * p['mb_conv_w'][j] for j in range(CONV_W))
    xbc = jax.nn.silu(conv.astype(F32))
    xs, Bm, Cm = split_cols(xbc, [C_C, G_C * N_C, G_C * N_C])
    dt = jax.nn.softplus((udt + p['mb_dt_bias']).astype(F32))
    A = -jnp.exp(p['mb_A_log'].astype(F32))
    xh = xs.reshape(b, L, G_C, E_C, P_C)
    dtg = dt.reshape(b, L, G_C, E_C)
    y, h_fin = ssd_scan(xh * dtg[..., None], dtg * A.reshape(G_C, E_C),
                        Bm.reshape(b, L, G_C, N_C), Cm.reshape(b, L, G_C, N_C),
                        ssm0.astype(F32).reshape(b, G_C, E_C, P_C, N_C))
    y = y + p['mb_D'].astype(F32).reshape(G_C, E_C)[:, :, None] * xh
    yz = (y.reshape(b, L, C_C) * jax.nn.silu(uz.astype(F32))).reshape(b, L, G_C, C_C // G_C)
    yz = yz * lax.rsqrt(jnp.mean(yz * yz, axis=-1, keepdims=True) + MB_NORM_EPS)
    out = yz.reshape(b, L, C_C) * p['mb_norm']
    return out.astype(uxbc.dtype), xpad[:, L:], h_fin.reshape(b, H_C, P_C, N_C)


def decoder_layer(x, p, lam_init, pos, shift0, wkv0, conv0, ssm0, attend):
    b, L, _ = x.shape
    x = x + 0.5 * swiglu(rms_norm(x, p['g_ffn1']), p['w_ffn1_gate'], p['w_ffn1_up'], p['w_ffn1_down'])
    h = rms_norm(x, p['g_mix'])
    u = h @ p['w_in']
    ua, uq, uk, uv, uz, uxbc, udt, ug = split_cols(
        u, [N_A_IN, C_QK, C_QK, C_B, C_C, CONV_DIM, H_C, N_GATES * D_MODEL])
    o_a, shift1, wkv1 = rwkv7_mixer(ua, shift0, wkv0, p)
    q, k, v = diff_qkv(uq, uk, uv, p, pos)
    lam = diff_lambda(p['da_lambda'], lam_init)
    o_b = diff_output(attend(q, k, v, lam, pos), p['da_subln'], lam_init)
    o_c, conv1, ssm1 = mamba2_mixer(uz, uxbc, udt, conv0, ssm0, p)
    gates = jax.nn.sigmoid(ug.astype(F32)).astype(x.dtype).reshape(b, L, N_GATES, D_MODEL)
    w_br = p['w_br']
    merged = (gates[:, :, 0] * (o_a @ w_br[:C_A])
              + gates[:, :, 1] * (o_b @ w_br[C_A:C_A + C_B])
              + gates[:, :, 2] * (o_c @ w_br[C_A + C_B:]))
    x = x + merged @ p['w_o']
    x = x + 0.5 * swiglu(rms_norm(x, p['g_ffn2']), p['w_ffn2_gate'], p['w_ffn2_up'], p['w_ffn2_down'])
    return x, (k.reshape(b, L, H_B, 2 * DH_B), v, shift1, wkv1, conv1, ssm1)


def setup_inputs(seed: int = 0) -> dict:
    key = jax.random.key(seed)
    keys = jax.random.split(key, 64)
    counter = [0]

    def nk():
        counter[0] += 1
        return keys[counter[0] - 1]

    def nrm(shape, scale=1.0):
        return jax.random.normal(nk(), shape, F32) * scale

    def unif(shape, lo, hi):
        return jax.random.uniform(nk(), shape, F32, lo, hi)

    n_pages = PAST_LEN // PAGE_SIZE
    n_used = DEC_BATCH * n_pages
    n_pool = n_used + (n_used + 3) // 4
    page_table = jax.random.permutation(nk(), n_pool)[:n_used].reshape(DEC_BATCH, n_pages).astype(jnp.int32)
    dt0 = jnp.exp(unif((DEPTH, H_C), math.log(1e-3), math.log(1e-1)))
    return {
        'x_prompt': nrm((BATCH, SEQ, D_MODEL)),
        'x_sample': nrm((DEC_BATCH, DEC_SEQ, D_MODEL)),
        'cache_k': nrm((DEPTH, n_pool, PAGE_SIZE, H_B, 2 * DH_B)),
        'cache_v': nrm((DEPTH, n_pool, PAGE_SIZE, H_B, VD_B)),
        'page_table': page_table,
        'state_rwkv_shift': nrm((DEPTH, DEC_BATCH, N_A_IN)),
        'state_rwkv_wkv': nrm((DEPTH, DEC_BATCH, H_A, HD_A, HD_A), 0.3),
        'state_conv': nrm((DEPTH, DEC_BATCH, CONV_W - 1, CONV_DIM)),
        'state_ssm': nrm((DEPTH, DEC_BATCH, H_C, P_C, N_C), 0.3),
        'g_ffn1': 1.0 + nrm((DEPTH, D_MODEL), 0.02),
        'w_ffn1_gate': nrm((DEPTH, D_MODEL, D_FF), D_MODEL ** -0.5),
        'w_ffn1_up': nrm((DEPTH, D_MODEL, D_FF), D_MODEL ** -0.5),
        'w_ffn1_down': nrm((DEPTH, D_FF, D_MODEL), D_FF ** -0.5),
        'g_mix': 1.0 + nrm((DEPTH, D_MODEL), 0.02),
        'w_in': nrm((DEPTH, D_MODEL, N_IN), D_MODEL ** -0.5),
        'rw_mu': unif((DEPTH, N_A_IN), 0.0, 1.0),
        'rw_w0': nrm((DEPTH, C_A), 0.5),
        'rw_w_up': nrm((DEPTH, LORA_W, C_A), LORA_W ** -0.5),
        'rw_a0': nrm((DEPTH, C_A), 0.5),
        'rw_a_up': nrm((DEPTH, LORA_A, C_A), LORA_A ** -0.5),
        'rw_g_up': nrm((DEPTH, LORA_G, C_A), LORA_G ** -0.5),
        'rw_k_k': 0.85 + nrm((DEPTH, C_A), 0.05),
        'rw_k_a': 1.0 + nrm((DEPTH, C_A), 0.05),
        'rw_r_k': nrm((DEPTH, H_A, HD_A), 0.1),
        'rw_ln_w': 1.0 + nrm((DEPTH, C_A), 0.02),
        'rw_ln_b': nrm((DEPTH, C_A), 0.02),
        'da_q_norm': 1.0 + nrm((DEPTH, 2, DH_B), 0.02),
        'da_k_norm': 1.0 + nrm((DEPTH, 2, DH_B), 0.02),
        'da_lambda': nrm((DEPTH, 4, DH_B), 0.1),
        'da_subln': 1.0 + nrm((DEPTH, VD_B), 0.02),
        'mb_conv_w': nrm((DEPTH, CONV_W, CONV_DIM), CONV_W ** -0.5),
        'mb_conv_b': nrm((DEPTH, CONV_DIM), 0.02),
        'mb_dt_bias': dt0 + jnp.log(-jnp.expm1(-dt0)),
        'mb_A_log': jnp.log(unif((DEPTH, H_C), 1.0, 16.0)),
        'mb_D': 1.0 + nrm((DEPTH, H_C), 0.1),
        'mb_norm': 1.0 + nrm((DEPTH, C_C), 0.02),
        'w_br': nrm((DEPTH, C_A + C_B + C_C, D_MODEL), (D_MODEL // 2) ** -0.5),
        'w_o': nrm((DEPTH, D_MODEL, D_MODEL), D_MODEL ** -0.5),
        'g_ffn2': 1.0 + nrm((DEPTH, D_MODEL), 0.02),
        'w_ffn2_gate': nrm((DEPTH, D_MODEL, D_FF), D_MODEL ** -0.5),
        'w_ffn2_up': nrm((DEPTH, D_MODEL, D_FF), D_MODEL ** -0.5),
        'w_ffn2_down': nrm((DEPTH, D_FF, D_MODEL), D_FF ** -0.5),
    }


def reference(x_prompt, x_sample, cache_k, cache_v, page_table, state_rwkv_shift, state_rwkv_wkv,
              state_conv, state_ssm, g_ffn1, w_ffn1_gate, w_ffn1_up, w_ffn1_down, g_mix, w_in,
              rw_mu, rw_w0, rw_w_up, rw_a0, rw_a_up, rw_g_up, rw_k_k, rw_k_a, rw_r_k, rw_ln_w, rw_ln_b,
              da_q_norm, da_k_norm, da_lambda, da_subln, mb_conv_w, mb_conv_b, mb_dt_bias, mb_A_log,
              mb_D, mb_norm, w_br, w_o, g_ffn2, w_ffn2_gate, w_ffn2_up, w_ffn2_down):
    stacked = dict(g_ffn1=g_ffn1, w_ffn1_gate=w_ffn1_gate, w_ffn1_up=w_ffn1_up, w_ffn1_down=w_ffn1_down,
                   g_mix=g_mix, w_in=w_in, rw_mu=rw_mu, rw_w0=rw_w0, rw_w_up=rw_w_up, rw_a0=rw_a0,
                   rw_a_up=rw_a_up, rw_g_up=rw_g_up, rw_k_k=rw_k_k, rw_k_a=rw_k_a, rw_r_k=rw_r_k,
                   rw_ln_w=rw_ln_w, rw_ln_b=rw_ln_b, da_q_norm=da_q_norm, da_k_norm=da_k_norm,
                   da_lambda=da_lambda, da_subln=da_subln, mb_conv_w=mb_conv_w, mb_conv_b=mb_conv_b,
                   mb_dt_bias=mb_dt_bias, mb_A_log=mb_A_log, mb_D=mb_D, mb_norm=mb_norm, w_br=w_br,
                   w_o=w_o, g_ffn2=g_ffn2, w_ffn2_gate=w_ffn2_gate, w_ffn2_up=w_ffn2_up,
                   w_ffn2_down=w_ffn2_down)
    bp, lp, _ = x_prompt.shape
    bd, ld, _ = x_sample.shape
    past = page_table.shape[1] * cache_k.shape[2]
    pos_p = jnp.arange(lp)
    pos_s = past + jnp.arange(ld)
    yp, ys = x_prompt, x_sample
    outs_p = [[] for _ in range(6)]
    outs_s = [[] for _ in range(6)]
    for l in range(DEPTH):
        p = {name: arr[l] for name, arr in stacked.items()}
        lam_init = 0.8 - 0.6 * math.exp(-0.3 * l)
        yp, st_p = decoder_layer(
            yp, p, lam_init, pos_p,
            jnp.zeros((bp, N_A_IN), x_prompt.dtype), jnp.zeros((bp, H_A, HD_A, HD_A), F32),
            jnp.zeros((bp, CONV_W - 1, CONV_DIM), x_prompt.dtype), jnp.zeros((bp, H_C, P_C, N_C), F32),
            attn_prompt)
        attend_s = functools.partial(attn_sample, k_cache=cache_k[l], v_cache=cache_v[l],
                                     page_table=page_table)
        ys, st_s = decoder_layer(
            ys, p, lam_init, pos_s,
            state_rwkv_shift[l], state_rwkv_wkv[l], state_conv[l], state_ssm[l], attend_s)
        for lst, s in zip(outs_p, st_p):
            lst.append(s)
        for lst, s in zip(outs_s, st_s):
            lst.append(s)
    k_p, v_p, shift_p, wkv_p, conv_p, ssm_p = [jnp.stack(o) for o in outs_p]
    k_s, v_s, shift_s, wkv_s, conv_s, ssm_s = [jnp.stack(o) for o in outs_s]
    return (yp, ys, k_p, v_p, shift_p, wkv_p, conv_p, ssm_p, k_s, v_s, shift_s, wkv_s, conv_s, ssm_s)
```

```python
import functools
import math

import jax
import jax.numpy as jnp
from jax import lax
from jax.experimental import pallas as pl
from jax.experimental.pallas import tpu as pltpu

F32 = jnp.float32
BF16 = jnp.bfloat16

NORM_EPS = 1e-6
ROPE_THETA = 10000.0
LN_X_EPS = 64e-5
MB_NORM_EPS = 1e-5

LANES = 128
BF16_ROWS = 16
MIB = 1024 * 1024
VMEM_BIG = 56 * MIB
VMEM_SMALL = 40 * MIB

RWKV_CHUNK = 64
SSD_CHUNK_T = 128
FFN_TF = 256


def _cparams(sem, vmem):
    return pltpu.CompilerParams(dimension_semantics=sem, vmem_limit_bytes=vmem)


def _round_up(n, m):
    return (n + m - 1) // m * m


def _pick_tile(n, cands):
    for c in cands:
        if n % c == 0:
            return c
    raise ValueError(f"no tile for {n} in {cands}")


def _dot(a, b):
    return jnp.dot(a, b, preferred_element_type=F32)


def _dot_nt(a, b):
    return lax.dot_general(a, b, (((1,), (1,)), ((), ())), preferred_element_type=F32)


def _softplus(x):
    return jnp.maximum(x, 0.0) + jnp.log1p(jnp.exp(-jnp.abs(x)))


def _silu(x):
    return x * jax.nn.sigmoid(x)


def _cumsum_rows(x):
    row = lax.broadcasted_iota(jnp.int32, x.shape, 0)
    shift = 1
    while shift < x.shape[0]:
        x = x + jnp.where(row >= shift, pltpu.roll(x, shift, axis=0), 0.0)
        shift *= 2
    return x


def _split_bf16(x):
    hi = x.astype(BF16)
    lo = (x - hi.astype(F32)).astype(BF16)
    return hi, lo


def _rmsnorm_body(x_ref, g_ref, o_ref):
    x = x_ref[...]
    ms = jnp.mean(x * x, axis=-1, keepdims=True)
    o_ref[...] = (x * lax.rsqrt(ms + NORM_EPS) * g_ref[...]).astype(BF16)


def _rmsnorm(x, g, l):
    m, d = x.shape
    tm = _pick_tile(m, (256, 128, 64, 32, 16))
    return pl.pallas_call(
        _rmsnorm_body,
        out_shape=jax.ShapeDtypeStruct((m, d), BF16),
        grid=(m // tm,),
        in_specs=[pl.BlockSpec((tm, d), lambda i: (i, 0)),
                  pl.BlockSpec((None, 1, d), lambda i: (l, 0, 0))],
        out_specs=pl.BlockSpec((tm, d), lambda i: (i, 0)),
        compiler_params=_cparams(("arbitrary",), VMEM_SMALL),
        name="rmsnorm",
    )(x, g)


def _ffn_body(x_ref, g_ref, wg_ref, wu_ref, wd_ref, o_ref, h_scr):
    @pl.when(pl.program_id(1) == 0)
    def _():
        x = x_ref[...]
        ms = jnp.mean(x * x, axis=-1, keepdims=True)
        h_scr[...] = (x * lax.rsqrt(ms + NORM_EPS) * g_ref[...]).astype(BF16)
        o_ref[...] = x

    h = h_scr[...]
    gate = _dot(h, wg_ref[...])
    up = _dot(h, wu_ref[...])
    act = (0.5 * _silu(gate) * up).astype(BF16)
    o_ref[...] += _dot(act, wd_ref[...])


def _ffn(x, g, wg, wu, wd, l):
    m, d = x.shape
    f = wg.shape[-1]
    tm = _pick_tile(m, (512, 256, 128, 64, 32, 16))
    tf = FFN_TF
    return pl.pallas_call(
        _ffn_body,
        out_shape=jax.ShapeDtypeStruct((m, d), F32),
        grid=(m // tm, f // tf),
        in_specs=[pl.BlockSpec((tm, d), lambda i, j: (i, 0), pipeline_mode=pl.Buffered(1)),
                  pl.BlockSpec((None, 1, d), lambda i, j: (l, 0, 0)),
                  pl.BlockSpec((None, d, tf), lambda i, j: (l, 0, j)),
                  pl.BlockSpec((None, d, tf), lambda i, j: (l, 0, j)),
                  pl.BlockSpec((None, tf, d), lambda i, j: (l, j, 0))],
        out_specs=pl.BlockSpec((tm, d), lambda i, j: (i, 0)),
        scratch_shapes=[pltpu.VMEM((tm, d), BF16)],
        compiler_params=_cparams(("arbitrary", "arbitrary"), VMEM_BIG),
        name="ffn",
    )(x, g, wg, wu, wd)


def _mm_body(x_ref, w_ref, *rest, residual, bf16_copy):
    acc = _dot(x_ref[...], w_ref[...])
    if residual:
        acc = rest[0][...] + acc
        rest = rest[1:]
    rest[0][...] = acc
    if bf16_copy:
        rest[1][...] = acc.astype(BF16)


def _mm(x, w, l, residual=None, bf16_copy=False):
    m, k = x.shape
    n = w.shape[-1]
    tm = _pick_tile(m, (1024, 512, 256, 128, 64, 32, 16))
    tn = _pick_tile(n, (1024, 512, 256, 128))
    in_specs = [pl.BlockSpec((tm, k), lambda i, j: (i, 0)),
                pl.BlockSpec((None, k, tn), lambda i, j: (l, 0, j))]
    args = [x, w]
    if residual is not None:
        in_specs.append(pl.BlockSpec((tm, tn), lambda i, j: (i, j)))
        args.append(residual)
    out_shape = [jax.ShapeDtypeStruct((m, n), F32)]
    out_specs = [pl.BlockSpec((tm, tn), lambda i, j: (i, j))]
    if bf16_copy:
        out_shape.append(jax.ShapeDtypeStruct((m, n), BF16))
        out_specs.append(pl.BlockSpec((tm, tn), lambda i, j: (i, j)))
    out = pl.pallas_call(
        functools.partial(_mm_body, residual=residual is not None, bf16_copy=bf16_copy),
        out_shape=out_shape,
        grid=(m // tm, n // tn),
        in_specs=in_specs,
        out_specs=out_specs,
        compiler_params=_cparams(("arbitrary", "arbitrary"), VMEM_BIG),
        name="matmul",
    )(*args)
    return out if bf16_copy else out[0]


def _merge_body(oa_ref, ob_ref, oc_ref, w_ref, g0_ref, g1_ref, g2_ref, o_ref):
    acc = jax.nn.sigmoid(g0_ref[...]) * _dot(oa_ref[...], w_ref[0])
    acc += jax.nn.sigmoid(g1_ref[...]) * _dot(ob_ref[...], w_ref[1])
    acc += jax.nn.sigmoid(g2_ref[...]) * _dot(oc_ref[...], w_ref[2])
    o_ref[...] = acc.astype(BF16)


def _merge(o_a, o_b, o_c, w_br3, ug, l):
    m, c = o_a.shape
    d = w_br3.shape[-1]
    tm = _pick_tile(m, (512, 256, 128, 64, 32, 16))
    tn = _pick_tile(d, (512, 256, 128))
    nj = d // tn
    o_spec = pl.BlockSpec((tm, c), lambda i, j: (i, 0))
    return pl.pallas_call(
        _merge_body,
        out_shape=jax.ShapeDtypeStruct((m, d), BF16),
        grid=(m // tm, nj),
        in_specs=[o_spec, o_spec, o_spec,
                  pl.BlockSpec((None, 3, c, tn), lambda i, j: (l, 0, 0, j)),
                  pl.BlockSpec((tm, tn), lambda i, j: (i, j)),
                  pl.BlockSpec((tm, tn), lambda i, j: (i, nj + j)),
                  pl.BlockSpec((tm, tn), lambda i, j: (i, 2 * nj + j))],
        out_specs=pl.BlockSpec((tm, tn), lambda i, j: (i, j)),
        compiler_params=_cparams(("arbitrary", "arbitrary"), VMEM_BIG),
        name="merge",
    )(o_a, o_b, o_c, w_br3, ug, ug, ug)


def _rwkv_prep_body(ua_ref, sh_ref, mu_ref, wl_ref, rkv_ref, lora_ref, carry, *, c3, lw, la):
    x = ua_ref[...]
    tm = x.shape[0]

    @pl.when(pl.program_id(1) == 0)
    def _():
        carry[...] = sh_ref[...]

    first_prev = carry[...]
    carry[...] = x[tm - 1:tm, :]
    row = lax.broadcasted_iota(jnp.int32, x.shape, 0)
    prev = jnp.where(row == 0, first_prev, pltpu.roll(x, 1, axis=0))
    us = x + mu_ref[...] * (prev - x)
    rkv_ref[...] = us[:, :c3]
    tail = us[:, c3:]
    lane = lax.broadcasted_iota(jnp.int32, tail.shape, 1)
    act = jnp.where(lane < lw, jnp.tanh(tail), jnp.where(lane < lw + la, tail, jax.nn.sigmoid(tail)))
    lora_ref[...] = _dot(act.astype(BF16), wl_ref[...])


def _rwkv_prep(ua3, shift0, mu, wl, l, c3, lw, la):
    b, lseq, nap = ua3.shape
    tm = _pick_tile(lseq, (128, 64))
    tail_w = nap - c3
    nblk = lseq // tm
    out_sds = jax.ShapeDtypeStruct((b * lseq, c3), F32)
    return pl.pallas_call(
        functools.partial(_rwkv_prep_body, c3=c3, lw=lw, la=la),
        out_shape=(out_sds, out_sds),
        grid=(b, nblk),
        in_specs=[pl.BlockSpec((None, tm, nap), lambda bi, i: (bi, i, 0)),
                  pl.BlockSpec((None, 1, nap), lambda bi, i: (bi, 0, 0)),
                  pl.BlockSpec((None, 1, nap), lambda bi, i: (l, 0, 0)),
                  pl.BlockSpec((None, tail_w, c3), lambda bi, i: (l, 0, 0))],
        out_specs=(pl.BlockSpec((tm, c3), lambda bi, i: (bi * nblk + i, 0)),
                   pl.BlockSpec((tm, c3), lambda bi, i: (bi * nblk + i, 0))),
        scratch_shapes=[pltpu.VMEM((1, nap), F32)],
        compiler_params=_cparams(("arbitrary", "arbitrary"), VMEM_BIG),
        name="rwkv_prep",
    )(ua3, shift0, mu, wl)


def _rwkv_scan_body(r_ref, k_ref, v_ref, wl_ref, al_ref, g_ref, par_ref, s0_ref, o_ref, so_ref, s_scr,
                    *, t, valid, nchunks, nseq):
    c = pl.program_id(2)

    @pl.when(c == 0)
    def _():
        s_scr[...] = s0_ref[...]

    par = par_ref[...]
    w0, a0, kk_w, ka_w, rk_w, ln_w, ln_b = (par[i:i + 1] for i in range(7))
    hd = LANES // 2
    lane = lax.broadcasted_iota(jnp.int32, (t, LANES), 1)
    m0 = lane < hd
    ri = lax.broadcasted_iota(jnp.int32, (2 * t, 2 * t), 0)
    ci = lax.broadcasted_iota(jnp.int32, (2 * t, 2 * t), 1)
    same_head = (ri // t) == (ci // t)
    strict = jnp.logical_and(same_head, ci < ri)
    incl = jnp.logical_and(same_head, ci <= ri)
    eye = (ci == ri).astype(F32)
    bi = lax.broadcasted_iota(jnp.int32, (LANES, LANES), 0) // hd
    bj = lax.broadcasted_iota(jnp.int32, (LANES, LANES), 1) // hd
    head_diag = bi == bj
    bf = lambda x: x.astype(BF16)
    twice = lambda x: jnp.concatenate([x, x], axis=0)

    def seg_sum(x):
        s_lo = jnp.sum(jnp.where(m0, x, 0.0), axis=1, keepdims=True)
        s_hi = jnp.sum(jnp.where(m0, 0.0, x), axis=1, keepdims=True)
        return jnp.where(m0, s_lo, s_hi)

    for i in range(nseq):
        r = r_ref[i]
        k = k_ref[i]
        v = v_ref[i]
        w = -_softplus(-(w0 + wl_ref[i])) - 0.5
        logd = -jnp.exp(w)
        a = jax.nn.sigmoid(a0 + al_ref[i])
        kkr = k * kk_w
        kk = kkr / jnp.maximum(jnp.sqrt(seg_sum(kkr * kkr)), 1e-12)
        k2 = k * (1.0 + (a - 1.0) * ka_w)
        if valid < t * nchunks:
            ok = c * t + lax.broadcasted_iota(jnp.int32, (t, LANES), 0) < valid
            logd = jnp.where(ok, logd, 0.0)
            kk = jnp.where(ok, kk, 0.0)
            k2m = jnp.where(ok, k2, 0.0)
            vm = jnp.where(ok, v, 0.0)
        else:
            k2m, vm = k2, v

        cs = _cumsum_rows(logd)
        gam = jnp.exp(cs)
        ginv = jnp.exp(-cs)
        rt = r * gam
        at = -kk * jnp.exp(cs - logd)
        bt = bf(kk * a * ginv)
        kt = bf(k2m * ginv)
        g_last = gam[t - 1:t, :]

        lhs = bf(jnp.concatenate([jnp.where(m0, at, 0.0), jnp.where(m0, 0.0, at),
                                  jnp.where(m0, rt, 0.0), jnp.where(m0, 0.0, rt)], axis=0))
        rhs = jnp.concatenate([bt, bt, kt, kt], axis=0)
        gram = _dot_nt(lhs, rhs)
        a_ab = jnp.where(strict, gram[:2 * t, :2 * t], 0.0)
        a_ak = jnp.where(strict, gram[:2 * t, 2 * t:], 0.0)
        m_rb = jnp.where(incl, gram[2 * t:, :2 * t], 0.0)
        m_rk = jnp.where(incl, gram[2 * t:, 2 * t:], 0.0)
        npow = a_ab
        inv = eye + npow
        for _ in range(int(math.log2(t)) - 1):
            npow = _dot(bf(npow), bf(npow))
            inv = inv + _dot(bf(inv), bf(npow))

        s_prev = s_scr[i]
        s_b = bf(s_prev)
        v2 = bf(twice(vm))
        w_st = _dot(bf(inv), bf(twice(_dot_nt(bf(at), s_b)) + _dot(bf(a_ak), v2)))
        y_st = twice(_dot_nt(bf(rt), s_b)) + _dot(bf(m_rb), bf(w_st)) + _dot(bf(m_rk), v2)
        wmat = jnp.where(m0, w_st[:t], w_st[t:])
        y = jnp.where(m0, y_st[:t], y_st[t:])

        x1 = jnp.concatenate([wmat, vm], axis=0)
        x2 = jnp.concatenate([bt, kt], axis=0)
        upd = _dot(bf(x1.T), x2)
        s_new = (s_prev + jnp.where(head_diag, upd, 0.0)) * g_last
        s_scr[i] = s_new

        inv_hd = 1.0 / hd
        yc = y - seg_sum(y) * inv_hd
        yn = yc * lax.rsqrt(seg_sum(yc * yc) * inv_hd + LN_X_EPS) * ln_w + ln_b
        bonus = seg_sum(r * k2 * rk_w) * v
        o_ref[i] = ((yn + bonus) * g_ref[i]).astype(BF16)

    @pl.when(c == nchunks - 1)
    def _():
        so_ref[...] = s_scr[...]


def _rwkv_scan(rkv, lora, par, s0, l, b, lseq, valid):
    c = rkv.shape[1] // 3
    npair = c // LANES
    t = RWKV_CHUNK
    nchunks = lseq // t
    nseq = _pick_tile(b, (4, 2, 1))
    rkv3 = rkv.reshape(b, lseq, 3 * c)
    lora3 = lora.reshape(b, lseq, 3 * c)

    def col(off):
        return pl.BlockSpec((nseq, t, LANES), lambda bi, p, ci: (bi, ci, off * npair + p))

    state = pl.BlockSpec((nseq, None, LANES, LANES), lambda bi, p, ci: (bi, p, 0, 0))
    o_a, s_out = pl.pallas_call(
        functools.partial(_rwkv_scan_body, t=t, valid=valid, nchunks=nchunks, nseq=nseq),
        out_shape=(jax.ShapeDtypeStruct((b, lseq, c), BF16),
                   jax.ShapeDtypeStruct((b, npair, LANES, LANES), F32)),
        grid=(b // nseq, npair, nchunks),
        in_specs=[col(0), col(1), col(2), col(0), col(1), col(2),
                  pl.BlockSpec((None, 8, LANES), lambda bi, p, ci: (l, 0, p)),
                  state],
        out_specs=(pl.BlockSpec((nseq, t, LANES), lambda bi, p, ci: (bi, ci, p)), state),
        scratch_shapes=[pltpu.VMEM((nseq, LANES, LANES), F32)],
        compiler_params=_cparams(("arbitrary", "arbitrary", "arbitrary"), VMEM_SMALL),
        name="rwkv_scan",
    )(rkv3, rkv3, rkv3, lora3, lora3, lora3, par, s0)
    return o_a.reshape(b * lseq, c), s_out


def _qk_prep_body(uq_ref, uk_ref, cos_ref, sin_ref, qn_ref, kn_ref, q_ref, k_ref, kb_ref, *, dh, scale):
    cos = cos_ref[...]
    sin = sin_ref[...]
    shape = cos.shape
    lane = lax.broadcasted_iota(jnp.int32, shape, 1)
    first_half = (lane % dh) < (dh // 2)
    si = lax.broadcasted_iota(jnp.int32, (LANES, LANES), 0) // dh
    sj = lax.broadcasted_iota(jnp.int32, (LANES, LANES), 1) // dh
    seg = (si == sj).astype(BF16)

    def prep(x, w):
        hi, lo = _split_bf16(x * x)
        ss = _dot(hi, seg) + _dot(lo, seg)
        y = x * lax.rsqrt(ss * (1.0 / dh) + NORM_EPS) * w
        partner = jnp.where(first_half, pltpu.roll(y, LANES - dh // 2, axis=1), pltpu.roll(y, dh // 2, axis=1))
        return y * cos + partner * sin

    q_ref[...] = prep(uq_ref[...], qn_ref[...]) * scale
    kr = prep(uk_ref[...], kn_ref[...])
    k_ref[...] = kr
    kb_ref[...] = kr.astype(BF16)


def _qk_prep(uq, uk, cos, sin, qn, kn, l, dh):
    m, cq = uq.shape
    tm = cos.shape[0]
    nh = cq // LANES
    nrep = m // tm
    blk = pl.BlockSpec((tm, LANES), lambda i, h: (i, h))
    tab = pl.BlockSpec((tm, LANES), lambda i, h: (0, 0))
    nrm = pl.BlockSpec((None, 1, LANES), lambda i, h: (l, 0, 0))
    return pl.pallas_call(
        functools.partial(_qk_prep_body, dh=dh, scale=dh ** -0.5),
        out_shape=(jax.ShapeDtypeStruct((m, cq), F32), jax.ShapeDtypeStruct((m, cq), F32),
                   jax.ShapeDtypeStruct((m, cq), BF16)),
        grid=(nrep, nh),
        in_specs=[blk, blk, tab, tab, nrm, nrm],
        out_specs=(blk, blk, blk),
        compiler_params=_cparams(("arbitrary", "arbitrary"), VMEM_SMALL),
        name="qk_prep",
    )(uq, uk, cos, sin, qn, kn)


def _diff_lambda(lp, lam_init):
    s1 = jnp.sum(lp[0:1] * lp[1:2], axis=1, keepdims=True)
    s2 = jnp.sum(lp[2:3] * lp[3:4], axis=1, keepdims=True)
    return jnp.exp(s1) - jnp.exp(s2) + lam_init


def _sub_ln(o, sub, lam_init):
    ms = jnp.mean(o * o, axis=-1, keepdims=True)
    return o * lax.rsqrt(ms + NORM_EPS) * sub * (1.0 - lam_init)


def _attn_prompt_body(q_ref, k_ref, v_ref, lam_ref, sub_ref, o_ref, *, lam_init, dh):
    qi = pl.program_id(2)
    q = q_ref[...]
    tq = q.shape[0]
    k = k_ref[...]
    lk = k.shape[0]
    lane = lax.broadcasted_iota(jnp.int32, q.shape, 1)
    q0 = jnp.where(lane < dh, q, 0.0).astype(BF16)
    q1 = jnp.where(lane < dh, 0.0, q).astype(BF16)
    qpos = qi * tq + lax.broadcasted_iota(jnp.int32, (tq, lk), 0)
    kpos = lax.broadcasted_iota(jnp.int32, (tq, lk), 1)
    causal = kpos <= qpos

    def softmax(s):
        s = jnp.where(causal, s, -jnp.inf)
        p = jnp.exp(s - jnp.max(s, axis=1, keepdims=True))
        return p * (1.0 / jnp.sum(p, axis=1, keepdims=True))

    lam = _diff_lambda(lam_ref[...], lam_init)
    a = softmax(_dot_nt(q0, k)) - lam * softmax(_dot_nt(q1, k))
    o = _dot(a.astype(BF16), v_ref[...])
    o_ref[...] = _sub_ln(o, sub_ref[...], lam_init).astype(BF16)


def _attn_prompt(q, kb, vb, lam_p, sub, l, b, lseq, lam_init, dh):
    m, cq = q.shape
    nh = cq // LANES
    tq = _pick_tile(lseq, (256, 128))
    nq = lseq // tq
    return pl.pallas_call(
        functools.partial(_attn_prompt_body, lam_init=lam_init, dh=dh),
        out_shape=jax.ShapeDtypeStruct((m, cq), BF16),
        grid=(b, nh, nq),
        in_specs=[pl.BlockSpec((tq, LANES), lambda bi, h, qi: (bi * nq + qi, h)),
                  pl.BlockSpec((lseq, LANES), lambda bi, h, qi: (bi, h)),
                  pl.BlockSpec((lseq, LANES), lambda bi, h, qi: (bi, h)),
                  pl.BlockSpec((None,) + lam_p.shape[1:], lambda bi, h, qi: (l, 0, 0)),
                  pl.BlockSpec((None, 1, LANES), lambda bi, h, qi: (l, 0, 0))],
        out_specs=pl.BlockSpec((tq, LANES), lambda bi, h, qi: (bi * nq + qi, h)),
        compiler_params=_cparams(("arbitrary", "arbitrary", "arbitrary"), VMEM_SMALL),
        name="attn_prompt",
    )(q, kb, vb, lam_p, sub)


def _attn_sample_body(pt_ref, q_ref, kc_ref, vc_ref, kcur_ref, vcur_ref, lam_ref, sub_ref, o_ref,
                      m_scr, l_scr, acc0, acc1, *, lam_init, dh, npages):
    del pt_ref
    p = pl.program_id(1)
    q = q_ref[...]
    nh = q.shape[0]
    ei = lax.broadcasted_iota(jnp.int32, (LANES, LANES), 0) // dh
    ej = lax.broadcasted_iota(jnp.int32, (LANES, LANES), 1)
    esel = (ei == ej).astype(BF16)

    def comp_sums(x):
        hi, lo = _split_bf16(x)
        return _dot(hi, esel) + _dot(lo, esel)

    @pl.when(p == 0)
    def _():
        m_scr[...] = comp_sums(q * kcur_ref[...])
        l_scr[...] = jnp.ones_like(l_scr)
        acc0[...] = vcur_ref[...]
        acc1[...] = vcur_ref[...]

    kp = kc_ref[...]
    ps = kp.shape[0]
    s = comp_sums((kp * q[None]).reshape(ps * nh, LANES))
    s3 = s.reshape(ps, nh, LANES)
    m_old = m_scr[...]
    m_new = jnp.maximum(m_old, jnp.max(s3, axis=0))
    alpha = jnp.exp(m_old - m_new)
    pexp = jnp.exp(s3 - m_new[None])
    l_scr[...] = alpha * l_scr[...] + jnp.sum(pexp, axis=0)
    m_scr[...] = m_new
    pexp2 = pexp.reshape(ps * nh, LANES)
    vp = vc_ref[...].reshape(ps * nh, LANES)
    for comp, acc in ((0, acc0), (1, acc1)):
        pc = jnp.broadcast_to(pexp2[:, comp:comp + 1], (ps * nh, LANES))
        ac = jnp.broadcast_to(alpha[:, comp:comp + 1], (nh, LANES))
        acc[...] = ac * acc[...] + jnp.sum((pc * vp).reshape(ps, nh, LANES), axis=0)

    @pl.when(p == npages - 1)
    def _():
        lsum = l_scr[...]
        l0 = jnp.broadcast_to(lsum[:, 0:1], (nh, LANES))
        l1 = jnp.broadcast_to(lsum[:, 1:2], (nh, LANES))
        lam = _diff_lambda(lam_ref[...], lam_init)
        o = acc0[...] * (1.0 / l0) - lam * (acc1[...] * (1.0 / l1))
        o_ref[...] = _sub_ln(o, sub_ref[...], lam_init)


def _attn_sample(page_table, q3, cache_k, cache_v, kcur3, vcur3, lam_p, sub, l, lam_init, dh):
    bs, nh, _ = q3.shape
    npages = page_table.shape[1]
    ps = cache_k.shape[2]
    cur = pl.BlockSpec((None, nh, LANES), lambda bi, p, pt: (bi, 0, 0))
    page = pl.BlockSpec((None, None, ps, nh, LANES), lambda bi, p, pt: (l, pt[bi, p], 0, 0, 0))
    grid_spec = pltpu.PrefetchScalarGridSpec(
        num_scalar_prefetch=1,
        grid=(bs, npages),
        in_specs=[cur, page, page, cur, cur,
                  pl.BlockSpec((None,) + lam_p.shape[1:], lambda bi, p, pt: (l, 0, 0)),
                  pl.BlockSpec((None, 1, LANES), lambda bi, p, pt: (l, 0, 0))],
        out_specs=cur,
        scratch_shapes=[pltpu.VMEM((nh, LANES), F32)] * 4,
    )
    return pl.pallas_call(
        functools.partial(_attn_sample_body, lam_init=lam_init, dh=dh, npages=npages),
        out_shape=jax.ShapeDtypeStruct((bs, nh, LANES), F32),
        grid_spec=grid_spec,
        compiler_params=_cparams(("arbitrary", "arbitrary"), VMEM_SMALL),
        name="attn_sample",
    )(page_table, q3, cache_k, cache_v, kcur3, vcur3, lam_p, sub)


def _conv_body(x_ref, c0_ref, w_ref, b_ref, o_ref, carry, *, width):
    x = x_ref[...]
    tm = x.shape[0]

    @pl.when(pl.program_id(1) == 0)
    def _():
        carry[...] = c0_ref[...]

    hist = carry[...]
    carry[...] = x[tm - 8:tm, :]
    w = w_ref[...]
    row8 = lax.broadcasted_iota(jnp.int32, hist.shape, 0)
    acc = b_ref[...] + x * w[width - 1:width]
    for j in range(1, width):
        rolled = pltpu.roll(x, j, axis=0)
        head = jnp.where(row8 < j, pltpu.roll(hist, j, axis=0), rolled[0:8])
        xj = jnp.concatenate([head, rolled[8:]], axis=0)
        acc = acc + xj * w[width - 1 - j:width - j]
    o_ref[...] = _silu(acc)


def _conv(x3, conv0_8, w, bias, l):
    b, lseq, cd = x3.shape
    width = w.shape[1]
    tm = _pick_tile(lseq, (256, 128))
    nblk = lseq // tm
    return pl.pallas_call(
        functools.partial(_conv_body, width=width),
        out_shape=jax.ShapeDtypeStruct((b * lseq, cd), F32),
        grid=(b, nblk),
        in_specs=[pl.BlockSpec((None, tm, cd), lambda bi, i: (bi, i, 0)),
                  pl.BlockSpec((None, 8, cd), lambda bi, i: (bi, 0, 0)),
                  pl.BlockSpec((None, width, cd), lambda bi, i: (l, 0, 0)),
                  pl.BlockSpec((None, 1, cd), lambda bi, i: (l, 0, 0))],
        out_specs=pl.BlockSpec((tm, cd), lambda bi, i: (bi * nblk + i, 0)),
        scratch_shapes=[pltpu.VMEM((8, cd), F32)],
        compiler_params=_cparams(("arbitrary", "arbitrary"), VMEM_SMALL),
        name="mamba_conv",
    )(x3, conv0_8, w, bias)


def _ssd_body(xbc_ref, z_ref, dt_ref, par_ref, nrm_ref, h0_ref, o_ref, ho_ref, st_scr,
              *, t, valid, nchunks, n_groups, heads_per_group, p_dim, n_dim):
    c = pl.program_id(1)

    @pl.when(c == 0)
    def _():
        st_scr[...] = h0_ref[...]

    par = par_ref[...]
    dt = _softplus(dt_ref[...] + par[0:1])
    if valid < t * nchunks:
        ok = c * t + lax.broadcasted_iota(jnp.int32, dt.shape, 0) < valid
        dt = jnp.where(ok, dt, 0.0)
    a_neg = -jnp.exp(par[1:2])
    d_skip = par[2:3]
    ri = lax.broadcasted_iota(jnp.int32, (t, t), 0)
    ci = lax.broadcasted_iota(jnp.int32, (t, t), 1)
    tril = ci <= ri
    acs = _cumsum_rows(dt * a_neg)
    acs_t = acs.T
    tot = acs[t - 1:t, :]
    lane = lax.broadcasted_iota(jnp.int32, (t, LANES), 1)
    m0 = lane < p_dim
    m0r = lax.broadcasted_iota(jnp.int32, (1, LANES), 1) < p_dim
    prow = lax.broadcasted_iota(jnp.int32, (LANES, n_dim), 0) < p_dim
    n_heads = n_groups * heads_per_group
    xs_w = n_heads * p_dim
    pairs_per_group = heads_per_group // 2
    gw = heads_per_group * p_dim

    for g in range(n_groups):
        b_g = xbc_ref[:, xs_w + g * n_dim: xs_w + (g + 1) * n_dim].astype(BF16)
        c_off = xs_w + n_groups * n_dim
        c_g = xbc_ref[:, c_off + g * n_dim: c_off + (g + 1) * n_dim].astype(BF16)
        cb = _dot_nt(c_g, b_g)
        ys = []
        for pp in range(pairs_per_group):
            pr = g * pairs_per_group + pp
            h_lo, h_hi = 2 * pr, 2 * pr + 1
            xs = xbc_ref[:, pr * LANES:(pr + 1) * LANES]
            col_lo, col_hi = acs[:, h_lo:h_lo + 1], acs[:, h_hi:h_hi + 1]
            l_lo = jnp.exp(jnp.where(tril, col_lo - acs_t[h_lo:h_lo + 1, :], -1e30))
            l_hi = jnp.exp(jnp.where(tril, col_hi - acs_t[h_hi:h_hi + 1, :], -1e30))
            x_dt = xs * jnp.where(m0, dt[:, h_lo:h_lo + 1], dt[:, h_hi:h_hi + 1])
            x_b = x_dt.astype(BF16)
            y_diag = jnp.where(m0, _dot((cb * l_lo).astype(BF16), x_b), _dot((cb * l_hi).astype(BF16), x_b))
            st = st_scr[pr]
            colp = jnp.where(m0, col_lo, col_hi)
            y_off = jnp.exp(colp) * _dot_nt(c_g, st.astype(BF16))
            totp = jnp.where(m0r, tot[:, h_lo:h_lo + 1], tot[:, h_hi:h_hi + 1])
            contrib = _dot((x_dt * jnp.exp(totp - colp)).T.astype(BF16), b_g)
            chunk_decay = jnp.where(prow, jnp.exp(tot[:, h_lo:h_lo + 1]), jnp.exp(tot[:, h_hi:h_hi + 1]))
            st_scr[pr] = chunk_decay * st + contrib
            dp = jnp.where(m0r, d_skip[:, h_lo:h_lo + 1], d_skip[:, h_hi:h_hi + 1])
            ys.append(y_diag + y_off + dp * xs)
        yg = ys[0] if len(ys) == 1 else jnp.concatenate(ys, axis=1)
        yz = yg * _silu(z_ref[:, g * gw:(g + 1) * gw])
        ms = jnp.mean(yz * yz, axis=-1, keepdims=True)
        o_ref[:, g * gw:(g + 1) * gw] = (yz * lax.rsqrt(ms + MB_NORM_EPS) * nrm_ref[:, g * gw:(g + 1) * gw]).astype(BF16)

    @pl.when(c == nchunks - 1)
    def _():
        ho_ref[...] = st_scr[...]


def _ssd(xbc, uz, udt, par, nrm, h0, l, b, lseq, valid, n_groups, heads_per_group, p_dim, n_dim):
    cd = xbc.shape[1]
    cc = uz.shape[1]
    t = SSD_CHUNK_T
    nchunks = lseq // t
    npair = h0.shape[1]
    return pl.pallas_call(
        functools.partial(_ssd_body, t=t, valid=valid, nchunks=nchunks, n_groups=n_groups,
                          heads_per_group=heads_per_group, p_dim=p_dim, n_dim=n_dim),
        out_shape=(jax.ShapeDtypeStruct((b * lseq, cc), BF16),
                   jax.ShapeDtypeStruct(h0.shape, F32)),
        grid=(b, nchunks),
        in_specs=[pl.BlockSpec((t, cd), lambda bi, ci: (bi * nchunks + ci, 0)),
                  pl.BlockSpec((t, cc), lambda bi, ci: (bi * nchunks + ci, 0)),
                  pl.BlockSpec((t, LANES), lambda bi, ci: (bi * nchunks + ci, 0)),
                  pl.BlockSpec((None, 8, LANES), lambda bi, ci: (l, 0, 0)),
                  pl.BlockSpec((None, 1, cc), lambda bi, ci: (l, 0, 0)),
                  pl.BlockSpec((None, npair, LANES, n_dim), lambda bi, ci: (bi, 0, 0, 0))],
        out_specs=(pl.BlockSpec((t, cc), lambda bi, ci: (bi * nchunks + ci, 0)),
                   pl.BlockSpec((None, npair, LANES, n_dim), lambda bi, ci: (bi, 0, 0, 0))),
        scratch_shapes=[pltpu.VMEM((npair, LANES, n_dim), F32)],
        compiler_params=_cparams(("arbitrary", "arbitrary"), VMEM_SMALL),
        name="ssd",
    )(xbc, uz, udt, par, nrm, h0)


def _pad_last(x, n):
    return jnp.pad(x, [(0, 0)] * (x.ndim - 1) + [(0, n - x.shape[-1])])


def _pad_rows(x, n):
    return jnp.pad(x, [(0, n - x.shape[0])] + [(0, 0)] * (x.ndim - 1))


def _wkv_to_blockdiag(wkv):
    b, h, hv, hk = wkv.shape
    w = wkv.reshape(b, h // 2, 2, hv, hk)
    z = jnp.zeros_like(w[:, :, 0])
    top = jnp.concatenate([w[:, :, 0], z], axis=-1)
    bot = jnp.concatenate([z, w[:, :, 1]], axis=-1)
    return jnp.concatenate([top, bot], axis=-2)


def _blockdiag_to_wkv(s, hv, hk):
    b, p = s.shape[:2]
    return jnp.stack([s[:, :, :hv, :hk], s[:, :, hv:, hk:]], axis=2).reshape(b, 2 * p, hv, hk)


def _first_token_rows(x_rows, bs, t):
    return jnp.pad(x_rows[:bs, None, :], ((0, 0), (0, t - 1), (0, 0)))


def kernel(x_prompt, x_sample, cache_k, cache_v, page_table, state_rwkv_shift, state_rwkv_wkv, state_conv, state_ssm, g_ffn1, w_ffn1_gate, w_ffn1_up, w_ffn1_down, g_mix, w_in, rw_mu, rw_w0, rw_w_up, rw_a0, rw_a_up, rw_g_up, rw_k_k, rw_k_a, rw_r_k, rw_ln_w, rw_ln_b, da_q_norm, da_k_norm, da_lambda, da_subln, mb_conv_w, mb_conv_b, mb_dt_bias, mb_A_log, mb_D, mb_norm, w_br, w_o, g_ffn2, w_ffn2_gate, w_ffn2_up, w_ffn2_down):
    bp, lp, d = x_prompt.shape
    bs, ls, _ = x_sample.shape
    depth = g_ffn1.shape[0]
    na = state_rwkv_shift.shape[-1]
    h_a, hd_a = state_rwkv_wkv.shape[2], state_rwkv_wkv.shape[3]
    c_a = h_a * hd_a
    lw, la, lg = rw_w_up.shape[1], rw_a_up.shape[1], rw_g_up.shape[1]
    h_b, cqk_h = cache_k.shape[3], cache_k.shape[4]
    vd_b = cache_v.shape[4]
    dh = cqk_h // 2
    c_qk, c_b = h_b * cqk_h, h_b * vd_b
    h_c, p_c, n_c = state_ssm.shape[2], state_ssm.shape[3], state_ssm.shape[4]
    c_c = h_c * p_c
    conv_w, conv_dim = mb_conv_w.shape[1], mb_conv_w.shape[2]
    g_c = (conv_dim - c_c) // (2 * n_c)
    e_c = h_c // g_c
    past = page_table.shape[1] * cache_k.shape[2]
    assert ls == 1, "sample group: one new token per sequence"
    assert hd_a * 2 == LANES and cqk_h == LANES and vd_b == LANES and p_c * 2 == LANES and n_c == LANES
    assert e_c % 2 == 0 and h_c <= LANES and conv_w <= 8
    assert na == 3 * c_a + lw + la + lg

    bf = lambda w: w.astype(BF16)
    nap = _round_up(na, LANES)
    seg_sizes = [na, c_qk, c_qk, c_b, c_c, conv_dim, h_c, 3 * d]
    offs = [0]
    for s in seg_sizes:
        offs.append(offs[-1] + s)
    assert offs[-1] == w_in.shape[-1]
    seg = lambda i: bf(w_in[:, :, offs[i]:offs[i + 1]])
    w_a = _pad_last(seg(0), nap)
    w_q, w_k, w_v, w_z, w_xbc = seg(1), seg(2), seg(3), seg(4), seg(5)
    w_dt = _pad_last(seg(6), LANES)
    w_g = seg(7)
    wg1, wu1, wd1 = bf(w_ffn1_gate), bf(w_ffn1_up), bf(w_ffn1_down)
    wg2, wu2, wd2 = bf(w_ffn2_gate), bf(w_ffn2_up), bf(w_ffn2_down)
    w_br3 = bf(w_br).reshape(depth, 3, c_a, d)
    w_o_b = bf(w_o)
    tail_w = nap - 3 * c_a
    w_lora = jnp.zeros((depth, tail_w, 3 * c_a), BF16)
    w_lora = w_lora.at[:, :lw, :c_a].set(bf(rw_w_up))
    w_lora = w_lora.at[:, lw:lw + la, c_a:2 * c_a].set(bf(rw_a_up))
    w_lora = w_lora.at[:, lw + la:lw + la + lg, 2 * c_a:].set(bf(rw_g_up))

    row3 = lambda x: x.reshape(depth, 1, -1)
    g1, gm, g2 = row3(g_ffn1), row3(g_mix), row3(g_ffn2)
    mu = row3(_pad_last(rw_mu, nap))
    rw_par = jnp.stack([rw_w0, rw_a0, rw_k_k, rw_k_a, rw_r_k.reshape(depth, c_a), rw_ln_w, rw_ln_b,
                        jnp.zeros_like(rw_w0)], axis=1)
    qn = row3(da_q_norm)
    kn = row3(da_k_norm)
    sub = row3(da_subln)
    mb_par = jnp.stack([_pad_last(mb_dt_bias, LANES), _pad_last(mb_A_log, LANES), _pad_last(mb_D, LANES)]
                       + [jnp.zeros((depth, LANES), F32)] * 5, axis=1)
    mb_nrm = row3(mb_norm)
    conv_b = row3(mb_conv_b)

    half = dh // 2
    inv = ROPE_THETA ** (-jnp.arange(half, dtype=F32) / half)

    def rope_tables(pos):
        ang = pos.astype(F32)[:, None] * inv[None, :]
        cos, sin = jnp.cos(ang), jnp.sin(ang)
        return jnp.tile(cos, (1, LANES // half)), jnp.tile(jnp.concatenate([-sin, sin], axis=1), (1, LANES // dh))

    cos_p, sin_p = rope_tables(jnp.arange(lp))
    ms_rows = _round_up(bs * ls, BF16_ROWS)
    cos_s, sin_s = rope_tables(jnp.full((ms_rows,), past, jnp.int32))

    xp = x_prompt.reshape(bp * lp, d)
    xs = _pad_rows(x_sample.reshape(bs * ls, d), ms_rows)
    t_a, t_c = RWKV_CHUNK, SSD_CHUNK_T
    outs_p = [[] for _ in range(6)]
    outs_s = [[] for _ in range(6)]

    def project(x, l):
        h = _rmsnorm(x, gm, l)
        ua = _mm(h, w_a, l)
        uq = _mm(h, w_q, l)
        uk = _mm(h, w_k, l)
        uv, uv_b = _mm(h, w_v, l, bf16_copy=True)
        uz = _mm(h, w_z, l)
        uxbc = _mm(h, w_xbc, l)
        udt = _mm(h, w_dt, l)
        ug = _mm(h, w_g, l)
        return ua, uq, uk, uv, uv_b, uz, uxbc, udt, ug

    def finish(x, o_a, o_b, o_c, ug, l):
        merged = _merge(o_a, o_b, o_c, w_br3, ug, l)
        x = _mm(merged, w_o_b, l, residual=x)
        return _ffn(x, g2, wg2, wu2, wd2, l)

    for l in range(depth):
        lam_init = 0.8 - 0.6 * math.exp(-0.3 * l)

        xp = _ffn(xp, g1, wg1, wu1, wd1, l)
        ua, uq, uk, uv, uv_b, uz, uxbc, udt, ug = project(xp, l)
        ua3 = ua.reshape(bp, lp, nap)
        rkv, lora = _rwkv_prep(ua3, jnp.zeros((bp, 1, nap), F32), mu, w_lora, l, 3 * c_a, lw, la)
        o_a, s_out = _rwkv_scan(rkv, lora, rw_par, jnp.zeros((bp, c_a // LANES, LANES, LANES), F32),
                                l, bp, lp, lp)
        q_r, k_r, k_rb = _qk_prep(uq, uk, cos_p, sin_p, qn, kn, l, dh)
        o_b = _attn_prompt(q_r, k_rb, uv_b, da_lambda, sub, l, bp, lp, lam_init, dh)
        xbc_act = _conv(uxbc.reshape(bp, lp, conv_dim), jnp.zeros((bp, 8, conv_dim), F32), mb_conv_w, conv_b, l)
        o_c, h_out = _ssd(xbc_act, uz, udt, mb_par, mb_nrm, jnp.zeros((bp, h_c // 2, LANES, n_c), F32),
                          l, bp, lp, lp, g_c, e_c, p_c, n_c)
        xp = finish(xp, o_a, o_b, o_c, ug, l)
        outs_p[0].append(k_r.reshape(bp, lp, h_b, cqk_h))
        outs_p[1].append(uv.reshape(bp, lp, h_b, vd_b))
        outs_p[2].append(ua3[:, lp - 1, :na])
        outs_p[3].append(_blockdiag_to_wkv(s_out, hd_a, hd_a))
        outs_p[4].append(uxbc.reshape(bp, lp, conv_dim)[:, lp - (conv_w - 1):])
        outs_p[5].append(h_out.reshape(bp, h_c, p_c, n_c))

        xs = _ffn(xs, g1, wg1, wu1, wd1, l)
        ua, uq, uk, uv, uv_b, uz, uxbc, udt, ug = project(xs, l)
        ua3 = _first_token_rows(ua, bs, t_a)
        shift0 = _pad_last(state_rwkv_shift[l], nap)[:, None, :]
        rkv, lora = _rwkv_prep(ua3, shift0, mu, w_lora, l, 3 * c_a, lw, la)
        o_a_seq, s_out = _rwkv_scan(rkv, lora, rw_par, _wkv_to_blockdiag(state_rwkv_wkv[l]), l, bs, t_a, ls)
        o_a = _pad_rows(o_a_seq.reshape(bs, t_a, c_a)[:, 0], ms_rows)
        q_r, k_r, _ = _qk_prep(uq, uk, cos_s, sin_s, qn, kn, l, dh)
        o_b3 = _attn_sample(page_table, q_r[:bs].reshape(bs, h_b, cqk_h), cache_k, cache_v,
                            k_r[:bs].reshape(bs, h_b, cqk_h), uv[:bs].reshape(bs, h_b, vd_b),
                            da_lambda, sub, l, lam_init, dh)
        o_b = _pad_rows(o_b3.reshape(bs, c_b), ms_rows).astype(BF16)
        conv0 = state_conv[l]
        conv0_8 = jnp.pad(conv0, ((0, 0), (8 - (conv_w - 1), 0), (0, 0)))
        xbc_act = _conv(_first_token_rows(uxbc, bs, t_c), conv0_8, mb_conv_w, conv_b, l)
        h0 = state_ssm[l].reshape(bs, h_c // 2, LANES, n_c)
        o_c_seq, h_out = _ssd(xbc_act, _first_token_rows(uz, bs, t_c).reshape(bs * t_c, c_c),
                              _first_token_rows(udt, bs, t_c).reshape(bs * t_c, LANES),
                              mb_par, mb_nrm, h0, l, bs, t_c, ls, g_c, e_c, p_c, n_c)
        o_c = _pad_rows(o_c_seq.reshape(bs, t_c, c_c)[:, 0], ms_rows)
        xs = finish(xs, o_a, o_b, o_c, ug, l)
        outs_s[0].append(k_r[:bs].reshape(bs, ls, h_b, cqk_h))
        outs_s[1].append(uv[:bs].reshape(bs, ls, h_b, vd_b))
        outs_s[2].append(ua[:bs, :na])
        outs_s[3].append(_blockdiag_to_wkv(s_out, hd_a, hd_a))
        outs_s[4].append(jnp.concatenate([conv0, uxbc[:bs, None, :]], axis=1)[:, ls:])
        outs_s[5].append(h_out.reshape(bs, h_c, p_c, n_c))

    yp = xp.reshape(bp, lp, d)
    ys = xs[:bs * ls].reshape(bs, ls, d)
    return (yp, ys) + tuple(jnp.stack(o) for o in outs_p) + tuple(jnp.stack(o) for o in outs_s)
```

```python
import functools
import math

import jax
import jax.numpy as jnp
from jax import lax
from jax.experimental import pallas as pl
from jax.experimental.pallas import tpu as pltpu

F32 = jnp.float32
BF16 = jnp.bfloat16

NORM_EPS = 1e-6
ROPE_THETA = 10000.0
LN_X_EPS = 64e-5
MB_NORM_EPS = 1e-5

LANES = 128
BF16_ROWS = 16
MIB = 1024 * 1024
VMEM_BIG = 56 * MIB
VMEM_SMALL = 40 * MIB

RWKV_CHAINS = 16
SAMPLE_PAGES_PER_STEP = 4
RWKV_CHUNK = 64
SSD_CHUNK_T = 128
FFN_TF = 256


def _cparams(sem, vmem):
    return pltpu.CompilerParams(dimension_semantics=sem, vmem_limit_bytes=vmem)


def _round_up(n, m):
    return (n + m - 1) // m * m


def _pick_tile(n, cands):
    for c in cands:
        if n % c == 0:
            return c
    raise ValueError(f"no tile for {n} in {cands}")


def _dot(a, b):
    return jnp.dot(a, b, preferred_element_type=F32)


def _dot_nt(a, b):
    return lax.dot_general(a, b, (((1,), (1,)), ((), ())), preferred_element_type=F32)


def _softplus(x):
    return jnp.maximum(x, 0.0) + jnp.log1p(jnp.exp(-jnp.abs(x)))


def _silu(x):
    return x * jax.nn.sigmoid(x)


def _cumsum_rows(x):
    row = lax.broadcasted_iota(jnp.int32, x.shape, 0)
    shift = 1
    while shift < x.shape[0]:
        x = x + jnp.where(row >= shift, pltpu.roll(x, shift, axis=0), 0.0)
        shift *= 2
    return x


def _split_bf16(x):
    hi = x.astype(BF16)
    lo = (x - hi.astype(F32)).astype(BF16)
    return hi, lo


def _rmsnorm_body(x_ref, g_ref, o_ref):
    x = x_ref[...]
    ms = jnp.mean(x * x, axis=-1, keepdims=True)
    o_ref[...] = (x * lax.rsqrt(ms + NORM_EPS) * g_ref[...]).astype(BF16)


def _rmsnorm(x, g, l):
    m, d = x.shape
    tm = _pick_tile(m, (256, 128, 64, 32, 16))
    return pl.pallas_call(
        _rmsnorm_body,
        out_shape=jax.ShapeDtypeStruct((m, d), BF16),
        grid=(m // tm,),
        in_specs=[pl.BlockSpec((tm, d), lambda i: (i, 0)),
                  pl.BlockSpec((None, 1, d), lambda i: (l, 0, 0))],
        out_specs=pl.BlockSpec((tm, d), lambda i: (i, 0)),
        compiler_params=_cparams(("arbitrary",), VMEM_SMALL),
        name="rmsnorm",
    )(x, g)


def _ffn_body(x_ref, g_ref, wg_ref, wu_ref, wd_ref, o_ref, h_scr):
    @pl.when(pl.program_id(1) == 0)
    def _():
        x = x_ref[...]
        ms = jnp.mean(x * x, axis=-1, keepdims=True)
        h_scr[...] = (x * lax.rsqrt(ms + NORM_EPS) * g_ref[...]).astype(BF16)
        o_ref[...] = x

    h = h_scr[...]
    gate = _dot(h, wg_ref[...])
    up = _dot(h, wu_ref[...])
    act = (0.5 * _silu(gate) * up).astype(BF16)
    o_ref[...] += _dot(act, wd_ref[...])


def _ffn(x, g, wg, wu, wd, l):
    m, d = x.shape
    f = wg.shape[-1]
    tm = _pick_tile(m, (512, 256, 128, 64, 32, 16))
    tf = FFN_TF
    return pl.pallas_call(
        _ffn_body,
        out_shape=jax.ShapeDtypeStruct((m, d), F32),
        grid=(m // tm, f // tf),
        in_specs=[pl.BlockSpec((tm, d), lambda i, j: (i, 0), pipeline_mode=pl.Buffered(1)),
                  pl.BlockSpec((None, 1, d), lambda i, j: (l, 0, 0)),
                  pl.BlockSpec((None, d, tf), lambda i, j: (l, 0, j)),
                  pl.BlockSpec((None, d, tf), lambda i, j: (l, 0, j)),
                  pl.BlockSpec((None, tf, d), lambda i, j: (l, j, 0))],
        out_specs=pl.BlockSpec((tm, d), lambda i, j: (i, 0)),
        scratch_shapes=[pltpu.VMEM((tm, d), BF16)],
        compiler_params=_cparams(("arbitrary", "arbitrary"), VMEM_BIG),
        name="ffn",
    )(x, g, wg, wu, wd)


def _mm_body(x_ref, w_ref, *rest, residual):
    acc = _dot(x_ref[...], w_ref[...])
    if residual:
        acc = rest[0][...] + acc
        rest = rest[1:]
    rest[0][...] = acc


def _mm(x, w, l, residual=None):
    m, k = x.shape
    n = w.shape[-1]
    tm = _pick_tile(m, (1024, 512, 256, 128, 64, 32, 16))
    tn = _pick_tile(n, (1024, 512, 256, 128))
    in_specs = [pl.BlockSpec((tm, k), lambda i, j: (i, 0)),
                pl.BlockSpec((None, k, tn), lambda i, j: (l, 0, j))]
    args = [x, w]
    if residual is not None:
        in_specs.append(pl.BlockSpec((tm, tn), lambda i, j: (i, j)))
        args.append(residual)
    return pl.pallas_call(
        functools.partial(_mm_body, residual=residual is not None),
        out_shape=jax.ShapeDtypeStruct((m, n), F32),
        grid=(m // tm, n // tn),
        in_specs=in_specs,
        out_specs=pl.BlockSpec((tm, tn), lambda i, j: (i, j)),
        compiler_params=_cparams(("arbitrary", "arbitrary"), VMEM_BIG),
        name="matmul",
    )(*args)


def _proj_body(x_ref, w_ref, o_ref, *rest):
    acc = _dot_nt(x_ref[...], w_ref[0].astype(BF16))
    o_ref[...] = acc
    if rest:
        rest[0][...] = acc.astype(BF16)


def _proj(x, w_t, l, start, n, bf16_copy=False):
    m, k = x.shape
    assert start % BF16_ROWS == 0 and start + n <= w_t.shape[1]
    tm = _pick_tile(m, (1024, 512, 256, 128, 64, 32, 16))
    tn = _pick_tile(n, (512, 256, 128))
    out_shape = [jax.ShapeDtypeStruct((m, n), F32)]
    out_specs = [pl.BlockSpec((tm, tn), lambda i, j: (i, j))]
    if bf16_copy:
        out_shape.append(jax.ShapeDtypeStruct((m, n), BF16))
        out_specs.append(pl.BlockSpec((tm, tn), lambda i, j: (i, j)))
    out = pl.pallas_call(
        _proj_body,
        out_shape=out_shape,
        grid=(m // tm, n // tn),
        in_specs=[pl.BlockSpec((tm, k), lambda i, j: (i, 0)),
                  pl.BlockSpec((pl.Element(1), pl.Element(tn), pl.Element(k)),
                               lambda i, j: (l, pl.multiple_of(start + j * tn, BF16_ROWS), 0))],
        out_specs=out_specs,
        compiler_params=_cparams(("arbitrary", "arbitrary"), VMEM_BIG),
        name="in_proj",
    )(x, w_t)
    return out if bf16_copy else out[0]


def _merge_body(oa_ref, ob_ref, oc_ref, w_ref, g0_ref, g1_ref, g2_ref, o_ref):
    acc = jax.nn.sigmoid(g0_ref[...]) * _dot(oa_ref[...], w_ref[0])
    acc += jax.nn.sigmoid(g1_ref[...]) * _dot(ob_ref[...], w_ref[1])
    acc += jax.nn.sigmoid(g2_ref[...]) * _dot(oc_ref[...], w_ref[2])
    o_ref[...] = acc.astype(BF16)


def _merge(o_a, o_b, o_c, w_br3, ug, l):
    m, c = o_a.shape
    d = w_br3.shape[-1]
    tm = _pick_tile(m, (512, 256, 128, 64, 32, 16))
    tn = _pick_tile(d, (512, 256, 128))
    nj = d // tn
    o_spec = pl.BlockSpec((tm, c), lambda i, j: (i, 0))
    return pl.pallas_call(
        _merge_body,
        out_shape=jax.ShapeDtypeStruct((m, d), BF16),
        grid=(m // tm, nj),
        in_specs=[o_spec, o_spec, o_spec,
                  pl.BlockSpec((None, 3, c, tn), lambda i, j: (l, 0, 0, j)),
                  pl.BlockSpec((tm, tn), lambda i, j: (i, j)),
                  pl.BlockSpec((tm, tn), lambda i, j: (i, nj + j)),
                  pl.BlockSpec((tm, tn), lambda i, j: (i, 2 * nj + j))],
        out_specs=pl.BlockSpec((tm, tn), lambda i, j: (i, j)),
        compiler_params=_cparams(("arbitrary", "arbitrary"), VMEM_BIG),
        name="merge",
    )(o_a, o_b, o_c, w_br3, ug, ug, ug)


def _rwkv_prep_body(ua_ref, sh_ref, mu_ref, wl_ref, rkv_ref, lora_ref, carry, *, c3, lw, la):
    x = ua_ref[...]
    tm = x.shape[0]

    @pl.when(pl.program_id(1) == 0)
    def _():
        carry[...] = sh_ref[...]

    first_prev = carry[...]
    carry[...] = x[tm - 1:tm, :]
    row = lax.broadcasted_iota(jnp.int32, x.shape, 0)
    prev = jnp.where(row == 0, first_prev, pltpu.roll(x, 1, axis=0))
    us = x + mu_ref[...] * (prev - x)
    rkv_ref[...] = us[:, :c3]
    tail = us[:, c3:]
    lane = lax.broadcasted_iota(jnp.int32, tail.shape, 1)
    act = jnp.where(lane < lw, jnp.tanh(tail), jnp.where(lane < lw + la, tail, jax.nn.sigmoid(tail)))
    lora_ref[...] = _dot(act.astype(BF16), wl_ref[...])


def _rwkv_prep(ua3, shift0, mu, wl, l, c3, lw, la):
    b, lseq, nap = ua3.shape
    tm = _pick_tile(lseq, (128, 64))
    tail_w = nap - c3
    nblk = lseq // tm
    out_sds = jax.ShapeDtypeStruct((b * lseq, c3), F32)
    return pl.pallas_call(
        functools.partial(_rwkv_prep_body, c3=c3, lw=lw, la=la),
        out_shape=(out_sds, out_sds),
        grid=(b, nblk),
        in_specs=[pl.BlockSpec((None, tm, nap), lambda bi, i: (bi, i, 0)),
                  pl.BlockSpec((None, 1, nap), lambda bi, i: (bi, 0, 0)),
                  pl.BlockSpec((None, 1, nap), lambda bi, i: (l, 0, 0)),
                  pl.BlockSpec((None, tail_w, c3), lambda bi, i: (l, 0, 0))],
        out_specs=(pl.BlockSpec((tm, c3), lambda bi, i: (bi * nblk + i, 0)),
                   pl.BlockSpec((tm, c3), lambda bi, i: (bi * nblk + i, 0))),
        scratch_shapes=[pltpu.VMEM((1, nap), F32)],
        compiler_params=_cparams(("arbitrary", "arbitrary"), VMEM_BIG),
        name="rwkv_prep",
    )(ua3, shift0, mu, wl)


def _rwkv_scan_body(r_ref, k_ref, v_ref, wl_ref, al_ref, g_ref, par_ref, s0_ref, o_ref, so_ref, s_scr,
                    *, t, valid, nchunks, nseq, npair):
    c = pl.program_id(2)

    @pl.when(c == 0)
    def _():
        s_scr[...] = s0_ref[...]

    hd = LANES // 2
    lane = lax.broadcasted_iota(jnp.int32, (t, LANES), 1)
    m0 = lane < hd
    ri = lax.broadcasted_iota(jnp.int32, (2 * t, 2 * t), 0)
    ci = lax.broadcasted_iota(jnp.int32, (2 * t, 2 * t), 1)
    same_head = (ri // t) == (ci // t)
    strict = jnp.logical_and(same_head, ci < ri)
    incl = jnp.logical_and(same_head, ci <= ri)
    eye = (ci == ri).astype(F32)
    bi = lax.broadcasted_iota(jnp.int32, (LANES, LANES), 0) // hd
    bj = lax.broadcasted_iota(jnp.int32, (LANES, LANES), 1) // hd
    head_diag = bi == bj
    bf = lambda x: x.astype(BF16)
    twice = lambda x: jnp.concatenate([x, x], axis=0)

    def seg_sum(x):
        s_lo = jnp.sum(jnp.where(m0, x, 0.0), axis=1, keepdims=True)
        s_hi = jnp.sum(jnp.where(m0, 0.0, x), axis=1, keepdims=True)
        return jnp.where(m0, s_lo, s_hi)

    chains = [(i, p) for i in range(nseq) for p in range(npair)]
    blk = lambda ref, i, p: ref[i, :, p * LANES:(p + 1) * LANES]

    def stage(f, *cols):
        return [f(*a) for a in zip(*cols)]

    def elementwise(i, p):
        par = par_ref[:, p * LANES:(p + 1) * LANES]
        w0, a0, kk_w, ka_w = (par[j:j + 1] for j in range(4))
        k = blk(k_ref, i, p)
        v = blk(v_ref, i, p)
        w = -_softplus(-(w0 + blk(wl_ref, i, p))) - 0.5
        logd = -jnp.exp(w)
        a = jax.nn.sigmoid(a0 + blk(al_ref, i, p))
        kkr = k * kk_w
        kk = kkr / jnp.maximum(jnp.sqrt(seg_sum(kkr * kkr)), 1e-12)
        k2 = k * (1.0 + (a - 1.0) * ka_w)
        if valid < t * nchunks:
            ok = c * t + lax.broadcasted_iota(jnp.int32, (t, LANES), 0) < valid
            logd = jnp.where(ok, logd, 0.0)
            kk = jnp.where(ok, kk, 0.0)
            k2m = jnp.where(ok, k2, 0.0)
            vm = jnp.where(ok, v, 0.0)
        else:
            k2m, vm = k2, v
        cs = _cumsum_rows(logd)
        gam = jnp.exp(cs)
        ginv = jnp.exp(-cs)
        rt = blk(r_ref, i, p) * gam
        at = -kk * jnp.exp(cs - logd)
        bt = bf(kk * a * ginv)
        kt = bf(k2m * ginv)
        return rt, at, bt, kt, vm, k2, gam[t - 1:t, :]

    rts, ats, bts, kts, vms, k2s, g_lasts = zip(*[elementwise(i, p) for i, p in chains])

    grams = stage(lambda at, rt, bt, kt: _dot_nt(
        bf(jnp.concatenate([jnp.where(m0, at, 0.0), jnp.where(m0, 0.0, at),
                            jnp.where(m0, rt, 0.0), jnp.where(m0, 0.0, rt)], axis=0)),
        jnp.concatenate([bt, bt, kt, kt], axis=0)), ats, rts, bts, kts)
    a_aks = stage(lambda g: bf(jnp.where(strict, g[:2 * t, 2 * t:], 0.0)), grams)
    m_rbs = stage(lambda g: bf(jnp.where(incl, g[2 * t:, :2 * t], 0.0)), grams)
    m_rks = stage(lambda g: bf(jnp.where(incl, g[2 * t:, 2 * t:], 0.0)), grams)
    npows = stage(lambda g: jnp.where(strict, g[:2 * t, :2 * t], 0.0), grams)
    invs = stage(lambda x: eye + x, npows)
    for _ in range(int(math.log2(t)) - 1):
        npows = stage(lambda x: _dot(bf(x), bf(x)), npows)
        invs = stage(lambda iv, x: iv + _dot(bf(iv), bf(x)), invs, npows)

    s_prevs = [s_scr[i, p] for i, p in chains]
    s_bs = stage(bf, s_prevs)
    v2s = stage(lambda vm: bf(twice(vm)), vms)
    zss = stage(lambda at, sb: _dot_nt(bf(at), sb), ats, s_bs)
    rss = stage(lambda rt, sb: _dot_nt(bf(rt), sb), rts, s_bs)
    rhss = stage(lambda zs, ak, v2: bf(twice(zs) + _dot(ak, v2)), zss, a_aks, v2s)
    w_sts = stage(lambda iv, x: _dot(bf(iv), x), invs, rhss)
    y_sts = stage(lambda rs, rb, w, rk, v2: twice(rs) + _dot(rb, bf(w)) + _dot(rk, v2), rss, m_rbs, w_sts, m_rks, v2s)
    wmats = stage(lambda w: jnp.where(m0, w[:t], w[t:]), w_sts)
    upds = stage(lambda w, vm, bt, kt: _dot(bf(jnp.concatenate([w, vm], axis=0).T), jnp.concatenate([bt, kt], axis=0)),
                 wmats, vms, bts, kts)
    for (i, p), s_prev, upd, g_last in zip(chains, s_prevs, upds, g_lasts):
        s_scr[i, p] = (s_prev + jnp.where(head_diag, upd, 0.0)) * g_last

    inv_hd = 1.0 / hd
    for (i, p), y_st, k2 in zip(chains, y_sts, k2s):
        par = par_ref[:, p * LANES:(p + 1) * LANES]
        rk_w, ln_w, ln_b = (par[j:j + 1] for j in range(4, 7))
        y = jnp.where(m0, y_st[:t], y_st[t:])
        yc = y - seg_sum(y) * inv_hd
        yn = yc * lax.rsqrt(seg_sum(yc * yc) * inv_hd + LN_X_EPS) * ln_w + ln_b
        bonus = seg_sum(blk(r_ref, i, p) * k2 * rk_w) * blk(v_ref, i, p)
        o_ref[i, :, p * LANES:(p + 1) * LANES] = ((yn + bonus) * blk(g_ref, i, p)).astype(BF16)

    @pl.when(c == nchunks - 1)
    def _():
        so_ref[...] = s_scr[...]


def _rwkv_scan(rkv, lora, par, s0, l, b, lseq, valid):
    c = rkv.shape[1] // 3
    t = RWKV_CHUNK
    nchunks = lseq // t
    nseq = _pick_tile(b, (4, 2, 1))
    npair = _pick_tile(c // LANES, (RWKV_CHAINS // nseq, 1))
    ngrp = c // (npair * LANES)
    rkv3 = rkv.reshape(b, lseq, 3 * c)
    lora3 = lora.reshape(b, lseq, 3 * c)

    def col(off):
        return pl.BlockSpec((nseq, t, npair * LANES), lambda bi, p, ci: (bi, ci, off * ngrp + p))

    state = pl.BlockSpec((nseq, npair, LANES, LANES), lambda bi, p, ci: (bi, p, 0, 0))
    o_a, s_out = pl.pallas_call(
        functools.partial(_rwkv_scan_body, t=t, valid=valid, nchunks=nchunks, nseq=nseq, npair=npair),
        out_shape=(jax.ShapeDtypeStruct((b, lseq, c), BF16),
                   jax.ShapeDtypeStruct((b, c // LANES, LANES, LANES), F32)),
        grid=(b // nseq, ngrp, nchunks),
        in_specs=[col(0), col(1), col(2), col(0), col(1), col(2),
                  pl.BlockSpec((None, 8, npair * LANES), lambda bi, p, ci: (l, 0, p)),
                  state],
        out_specs=(pl.BlockSpec((nseq, t, npair * LANES), lambda bi, p, ci: (bi, ci, p)), state),
        scratch_shapes=[pltpu.VMEM((nseq, npair, LANES, LANES), F32)],
        compiler_params=_cparams(("arbitrary", "arbitrary", "arbitrary"), VMEM_SMALL),
        name="rwkv_scan",
    )(rkv3, rkv3, rkv3, lora3, lora3, lora3, par, s0)
    return o_a.reshape(b * lseq, c), s_out


def _qk_prep_body(uq_ref, uk_ref, cos_ref, sin_ref, qn_ref, kn_ref, q_ref, k_ref, kb_ref, *, dh, scale):
    cos = cos_ref[...]
    sin = sin_ref[...]
    shape = cos.shape
    lane = lax.broadcasted_iota(jnp.int32, shape, 1)
    first_half = (lane % dh) < (dh // 2)
    si = lax.broadcasted_iota(jnp.int32, (LANES, LANES), 0) // dh
    sj = lax.broadcasted_iota(jnp.int32, (LANES, LANES), 1) // dh
    seg = (si == sj).astype(BF16)

    def prep(x, w):
        hi, lo = _split_bf16(x * x)
        ss = _dot(hi, seg) + _dot(lo, seg)
        y = x * lax.rsqrt(ss * (1.0 / dh) + NORM_EPS) * w
        partner = jnp.where(first_half, pltpu.roll(y, LANES - dh // 2, axis=1), pltpu.roll(y, dh // 2, axis=1))
        return y * cos + partner * sin

    q_ref[...] = prep(uq_ref[...], qn_ref[...]) * scale
    kr = prep(uk_ref[...], kn_ref[...])
    k_ref[...] = kr
    kb_ref[...] = kr.astype(BF16)


def _qk_prep(uq, uk, cos, sin, qn, kn, l, dh):
    m, cq = uq.shape
    tm = cos.shape[0]
    nh = cq // LANES
    nrep = m // tm
    blk = pl.BlockSpec((tm, LANES), lambda i, h: (i, h))
    tab = pl.BlockSpec((tm, LANES), lambda i, h: (0, 0))
    nrm = pl.BlockSpec((None, 1, LANES), lambda i, h: (l, 0, 0))
    return pl.pallas_call(
        functools.partial(_qk_prep_body, dh=dh, scale=dh ** -0.5),
        out_shape=(jax.ShapeDtypeStruct((m, cq), F32), jax.ShapeDtypeStruct((m, cq), F32),
                   jax.ShapeDtypeStruct((m, cq), BF16)),
        grid=(nrep, nh),
        in_specs=[blk, blk, tab, tab, nrm, nrm],
        out_specs=(blk, blk, blk),
        compiler_params=_cparams(("arbitrary", "arbitrary"), VMEM_SMALL),
        name="qk_prep",
    )(uq, uk, cos, sin, qn, kn)


def _diff_lambda(lp, lam_init):
    s1 = jnp.sum(lp[0:1] * lp[1:2], axis=1, keepdims=True)
    s2 = jnp.sum(lp[2:3] * lp[3:4], axis=1, keepdims=True)
    return jnp.exp(s1) - jnp.exp(s2) + lam_init


def _sub_ln(o, sub, lam_init):
    ms = jnp.mean(o * o, axis=-1, keepdims=True)
    return o * lax.rsqrt(ms + NORM_EPS) * sub * (1.0 - lam_init)


def _attn_prompt_body(q_ref, k_ref, v_ref, lam_ref, sub_ref, o_ref, *, lam_init, dh, tq):
    lseq = q_ref.shape[0]
    lane = lax.broadcasted_iota(jnp.int32, (tq, LANES), 1)
    tri = lax.broadcasted_iota(jnp.int32, (tq, tq), 1) <= lax.broadcasted_iota(jnp.int32, (tq, tq), 0)
    lam = _diff_lambda(lam_ref[...], lam_init)
    sub = sub_ref[...]
    for qi in range(lseq // tq):
        lo, hi = qi * tq, (qi + 1) * tq
        q = q_ref[lo:hi, :]
        k_diag = k_ref[lo:hi, :]
        v_diag = v_ref[lo:hi, :]
        outs = []
        for qc in (jnp.where(lane < dh, q, 0.0).astype(BF16), jnp.where(lane < dh, 0.0, q).astype(BF16)):
            s_diag = jnp.where(tri, _dot_nt(qc, k_diag), -jnp.inf)
            m = jnp.max(s_diag, axis=1, keepdims=True)
            if qi:
                s_off = _dot_nt(qc, k_ref[0:lo, :])
                m = jnp.maximum(m, jnp.max(s_off, axis=1, keepdims=True))
            p_diag = jnp.exp(s_diag - m)
            l = jnp.sum(p_diag, axis=1, keepdims=True)
            o = _dot(p_diag.astype(BF16), v_diag)
            if qi:
                p_off = jnp.exp(s_off - m)
                l = l + jnp.sum(p_off, axis=1, keepdims=True)
                o = o + _dot(p_off.astype(BF16), v_ref[0:lo, :])
            outs.append(o * (1.0 / l))
        o_ref[lo:hi, :] = _sub_ln(outs[0] - lam * outs[1], sub, lam_init).astype(BF16)


def _attn_prompt(q, kb, vb, lam_p, sub, l, b, lseq, lam_init, dh):
    m, cq = q.shape
    nh = cq // LANES
    tq = _pick_tile(lseq, (256, 128))
    blk = pl.BlockSpec((lseq, LANES), lambda bi, h: (bi, h))
    return pl.pallas_call(
        functools.partial(_attn_prompt_body, lam_init=lam_init, dh=dh, tq=tq),
        out_shape=jax.ShapeDtypeStruct((m, cq), BF16),
        grid=(b, nh),
        in_specs=[blk, blk, blk,
                  pl.BlockSpec((None,) + lam_p.shape[1:], lambda bi, h: (l, 0, 0)),
                  pl.BlockSpec((None, 1, LANES), lambda bi, h: (l, 0, 0))],
        out_specs=blk,
        compiler_params=_cparams(("arbitrary", "arbitrary"), VMEM_SMALL),
        name="attn_prompt",
    )(q, kb, vb, lam_p, sub)


def _attn_sample_body(pt_ref, q_ref, *rest, lam_init, dh, nsteps, npp):
    del pt_ref
    kc_refs, vc_refs = rest[:npp], rest[npp:2 * npp]
    kcur_ref, vcur_ref, lam_ref, sub_ref, o_ref, m_scr, l_scr, acc0, acc1 = rest[2 * npp:]
    p = pl.program_id(1)
    q = q_ref[...]
    nh = q.shape[0]
    ei = lax.broadcasted_iota(jnp.int32, (LANES, LANES), 0) // dh
    ej = lax.broadcasted_iota(jnp.int32, (LANES, LANES), 1)
    esel = (ei == ej).astype(BF16)

    def comp_sums(x):
        hi, lo = _split_bf16(x)
        return _dot(hi, esel) + _dot(lo, esel)

    @pl.when(p == 0)
    def _():
        m_scr[...] = comp_sums(q * kcur_ref[...])
        l_scr[...] = jnp.ones_like(l_scr)
        acc0[...] = vcur_ref[...]
        acc1[...] = vcur_ref[...]

    ps = kc_refs[0].shape[0]
    rows = ps * nh
    s3s = [comp_sums((kc[...] * q[None]).reshape(rows, LANES)).reshape(ps, nh, LANES) for kc in kc_refs]
    m_old = m_scr[...]
    m_new = m_old
    for s3 in s3s:
        m_new = jnp.maximum(m_new, jnp.max(s3, axis=0))
    alpha = jnp.exp(m_old - m_new)
    pexps = [jnp.exp(s3 - m_new[None]) for s3 in s3s]
    l_new = alpha * l_scr[...]
    for pexp in pexps:
        l_new = l_new + jnp.sum(pexp, axis=0)
    l_scr[...] = l_new
    m_scr[...] = m_new
    pexp2s = [pexp.reshape(rows, LANES) for pexp in pexps]
    vps = [vc[...].reshape(rows, LANES) for vc in vc_refs]
    for comp, acc in ((0, acc0), (1, acc1)):
        tot = jnp.broadcast_to(alpha[:, comp:comp + 1], (nh, LANES)) * acc[...]
        for pexp2, vp in zip(pexp2s, vps):
            pc = jnp.broadcast_to(pexp2[:, comp:comp + 1], (rows, LANES))
            tot = tot + jnp.sum((pc * vp).reshape(ps, nh, LANES), axis=0)
        acc[...] = tot

    @pl.when(p == nsteps - 1)
    def _():
        lsum = l_scr[...]
        l0 = jnp.broadcast_to(lsum[:, 0:1], (nh, LANES))
        l1 = jnp.broadcast_to(lsum[:, 1:2], (nh, LANES))
        lam = _diff_lambda(lam_ref[...], lam_init)
        o = acc0[...] * (1.0 / l0) - lam * (acc1[...] * (1.0 / l1))
        o_ref[...] = _sub_ln(o, sub_ref[...], lam_init)


def _attn_sample(page_table, q3, cache_k, cache_v, kcur3, vcur3, lam_p, sub, l, lam_init, dh):
    bs, nh, _ = q3.shape
    npages = page_table.shape[1]
    ps = cache_k.shape[2]
    npp = _pick_tile(npages, (SAMPLE_PAGES_PER_STEP, 2, 1))
    nsteps = npages // npp
    cur = pl.BlockSpec((None, nh, LANES), lambda bi, p, pt: (bi, 0, 0))

    def page(j):
        return pl.BlockSpec((None, None, ps, nh, LANES), lambda bi, p, pt: (l, pt[bi, p * npp + j], 0, 0, 0))

    pages = [page(j) for j in range(npp)]
    grid_spec = pltpu.PrefetchScalarGridSpec(
        num_scalar_prefetch=1,
        grid=(bs, nsteps),
        in_specs=[cur] + pages + pages + [
            cur, cur,
            pl.BlockSpec((None,) + lam_p.shape[1:], lambda bi, p, pt: (l, 0, 0)),
            pl.BlockSpec((None, 1, LANES), lambda bi, p, pt: (l, 0, 0))],
        out_specs=cur,
        scratch_shapes=[pltpu.VMEM((nh, LANES), F32)] * 4,
    )
    return pl.pallas_call(
        functools.partial(_attn_sample_body, lam_init=lam_init, dh=dh, nsteps=nsteps, npp=npp),
        out_shape=jax.ShapeDtypeStruct((bs, nh, LANES), F32),
        grid_spec=grid_spec,
        compiler_params=_cparams(("arbitrary", "arbitrary"), VMEM_SMALL),
        name="attn_sample",
    )(page_table, q3, *([cache_k] * npp), *([cache_v] * npp), kcur3, vcur3, lam_p, sub)


def _conv_body(x_ref, c0_ref, w_ref, b_ref, o_ref, carry, *, width):
    x = x_ref[...]
    tm = x.shape[0]

    @pl.when(pl.program_id(1) == 0)
    def _():
        carry[...] = c0_ref[...]

    hist = carry[...]
    carry[...] = x[tm - 8:tm, :]
    w = w_ref[...]
    row8 = lax.broadcasted_iota(jnp.int32, hist.shape, 0)
    acc = b_ref[...] + x * w[width - 1:width]
    for j in range(1, width):
        rolled = pltpu.roll(x, j, axis=0)
        head = jnp.where(row8 < j, pltpu.roll(hist, j, axis=0), rolled[0:8])
        xj = jnp.concatenate([head, rolled[8:]], axis=0)
        acc = acc + xj * w[width - 1 - j:width - j]
    o_ref[...] = _silu(acc)


def _conv(x3, conv0_8, w, bias, l):
    b, lseq, cd = x3.shape
    width = w.shape[1]
    tm = _pick_tile(lseq, (256, 128))
    nblk = lseq // tm
    return pl.pallas_call(
        functools.partial(_conv_body, width=width),
        out_shape=jax.ShapeDtypeStruct((b * lseq, cd), F32),
        grid=(b, nblk),
        in_specs=[pl.BlockSpec((None, tm, cd), lambda bi, i: (bi, i, 0)),
                  pl.BlockSpec((None, 8, cd), lambda bi, i: (bi, 0, 0)),
                  pl.BlockSpec((None, width, cd), lambda bi, i: (l, 0, 0)),
                  pl.BlockSpec((None, 1, cd), lambda bi, i: (l, 0, 0))],
        out_specs=pl.BlockSpec((tm, cd), lambda bi, i: (bi * nblk + i, 0)),
        scratch_shapes=[pltpu.VMEM((8, cd), F32)],
        compiler_params=_cparams(("arbitrary", "arbitrary"), VMEM_SMALL),
        name="mamba_conv",
    )(x3, conv0_8, w, bias)


def _ssd_body(xbc_ref, z_ref, dt_ref, par_ref, nrm_ref, h0_ref, o_ref, ho_ref, st_scr,
              *, t, valid, nchunks, n_groups, heads_per_group, p_dim, n_dim):
    c = pl.program_id(1)

    @pl.when(c == 0)
    def _():
        st_scr[...] = h0_ref[...]

    par = par_ref[...]
    dt = _softplus(dt_ref[...] + par[0:1])
    if valid < t * nchunks:
        ok = c * t + lax.broadcasted_iota(jnp.int32, dt.shape, 0) < valid
        dt = jnp.where(ok, dt, 0.0)
    a_neg = -jnp.exp(par[1:2])
    d_skip = par[2:3]
    ri = lax.broadcasted_iota(jnp.int32, (t, t), 0)
    ci = lax.broadcasted_iota(jnp.int32, (t, t), 1)
    tril = ci <= ri
    acs = _cumsum_rows(dt * a_neg)
    acs_t = acs.T
    tot = acs[t - 1:t, :]
    lane = lax.broadcasted_iota(jnp.int32, (t, LANES), 1)
    m0 = lane < p_dim
    m0r = lax.broadcasted_iota(jnp.int32, (1, LANES), 1) < p_dim
    prow = lax.broadcasted_iota(jnp.int32, (LANES, n_dim), 0) < p_dim
    n_heads = n_groups * heads_per_group
    xs_w = n_heads * p_dim
    pairs_per_group = heads_per_group // 2
    gw = heads_per_group * p_dim

    for g in range(n_groups):
        b_g = xbc_ref[:, xs_w + g * n_dim: xs_w + (g + 1) * n_dim].astype(BF16)
        c_off = xs_w + n_groups * n_dim
        c_g = xbc_ref[:, c_off + g * n_dim: c_off + (g + 1) * n_dim].astype(BF16)
        cb = _dot_nt(c_g, b_g)
        ys = []
        for pp in range(pairs_per_group):
            pr = g * pairs_per_group + pp
            h_lo, h_hi = 2 * pr, 2 * pr + 1
            xs = xbc_ref[:, pr * LANES:(pr + 1) * LANES]
            col_lo, col_hi = acs[:, h_lo:h_lo + 1], acs[:, h_hi:h_hi + 1]
            l_lo = jnp.exp(jnp.where(tril, col_lo - acs_t[h_lo:h_lo + 1, :], -1e30))
            l_hi = jnp.exp(jnp.where(tril, col_hi - acs_t[h_hi:h_hi + 1, :], -1e30))
            x_dt = xs * jnp.where(m0, dt[:, h_lo:h_lo + 1], dt[:, h_hi:h_hi + 1])
            x_b = x_dt.astype(BF16)
            y_diag = jnp.where(m0, _dot((cb * l_lo).astype(BF16), x_b), _dot((cb * l_hi).astype(BF16), x_b))
            st = st_scr[pr]
            colp = jnp.where(m0, col_lo, col_hi)
            y_off = jnp.exp(colp) * _dot_nt(c_g, st.astype(BF16))
            totp = jnp.where(m0r, tot[:, h_lo:h_lo + 1], tot[:, h_hi:h_hi + 1])
            contrib = _dot((x_dt * jnp.exp(totp - colp)).T.astype(BF16), b_g)
            chunk_decay = jnp.where(prow, jnp.exp(tot[:, h_lo:h_lo + 1]), jnp.exp(tot[:, h_hi:h_hi + 1]))
            st_scr[pr] = chunk_decay * st + contrib
            dp = jnp.where(m0r, d_skip[:, h_lo:h_lo + 1], d_skip[:, h_hi:h_hi + 1])
            ys.append(y_diag + y_off + dp * xs)
        yg = ys[0] if len(ys) == 1 else jnp.concatenate(ys, axis=1)
        yz = yg * _silu(z_ref[:, g * gw:(g + 1) * gw])
        ms = jnp.mean(yz * yz, axis=-1, keepdims=True)
        o_ref[:, g * gw:(g + 1) * gw] = (yz * lax.rsqrt(ms + MB_NORM_EPS) * nrm_ref[:, g * gw:(g + 1) * gw]).astype(BF16)

    @pl.when(c == nchunks - 1)
    def _():
        ho_ref[...] = st_scr[...]


def _ssd(xbc, uz, udt, par, nrm, h0, l, b, lseq, valid, n_groups, heads_per_group, p_dim, n_dim):
    cd = xbc.shape[1]
    cc = uz.shape[1]
    t = SSD_CHUNK_T
    nchunks = lseq // t
    npair = h0.shape[1]
    return pl.pallas_call(
        functools.partial(_ssd_body, t=t, valid=valid, nchunks=nchunks, n_groups=n_groups,
                          heads_per_group=heads_per_group, p_dim=p_dim, n_dim=n_dim),
        out_shape=(jax.ShapeDtypeStruct((b * lseq, cc), BF16),
                   jax.ShapeDtypeStruct(h0.shape, F32)),
        grid=(b, nchunks),
        in_specs=[pl.BlockSpec((t, cd), lambda bi, ci: (bi * nchunks + ci, 0)),
                  pl.BlockSpec((t, cc), lambda bi, ci: (bi * nchunks + ci, 0)),
                  pl.BlockSpec((t, LANES), lambda bi, ci: (bi * nchunks + ci, 0)),
                  pl.BlockSpec((None, 8, LANES), lambda bi, ci: (l, 0, 0)),
                  pl.BlockSpec((None, 1, cc), lambda bi, ci: (l, 0, 0)),
                  pl.BlockSpec((None, npair, LANES, n_dim), lambda bi, ci: (bi, 0, 0, 0))],
        out_specs=(pl.BlockSpec((t, cc), lambda bi, ci: (bi * nchunks + ci, 0)),
                   pl.BlockSpec((None, npair, LANES, n_dim), lambda bi, ci: (bi, 0, 0, 0))),
        scratch_shapes=[pltpu.VMEM((npair, LANES, n_dim), F32)],
        compiler_params=_cparams(("arbitrary", "arbitrary"), VMEM_SMALL),
        name="ssd",
    )(xbc, uz, udt, par, nrm, h0)


def _pad_last(x, n):
    return jnp.pad(x, [(0, 0)] * (x.ndim - 1) + [(0, n - x.shape[-1])])


def _pad_rows(x, n):
    return jnp.pad(x, [(0, n - x.shape[0])] + [(0, 0)] * (x.ndim - 1))


def _wkv_to_blockdiag(wkv):
    b, h, hv, hk = wkv.shape
    w = wkv.reshape(b, h // 2, 2, hv, hk)
    z = jnp.zeros_like(w[:, :, 0])
    top = jnp.concatenate([w[:, :, 0], z], axis=-1)
    bot = jnp.concatenate([z, w[:, :, 1]], axis=-1)
    return jnp.concatenate([top, bot], axis=-2)


def _blockdiag_to_wkv(s, hv, hk):
    b, p = s.shape[:2]
    return jnp.stack([s[:, :, :hv, :hk], s[:, :, hv:, hk:]], axis=2).reshape(b, 2 * p, hv, hk)


def _first_token_rows(x_rows, bs, t):
    return jnp.pad(x_rows[:bs, None, :], ((0, 0), (0, t - 1), (0, 0)))


def kernel(x_prompt, x_sample, cache_k, cache_v, page_table, state_rwkv_shift, state_rwkv_wkv, state_conv, state_ssm, g_ffn1, w_ffn1_gate, w_ffn1_up, w_ffn1_down, g_mix, w_in, rw_mu, rw_w0, rw_w_up, rw_a0, rw_a_up, rw_g_up, rw_k_k, rw_k_a, rw_r_k, rw_ln_w, rw_ln_b, da_q_norm, da_k_norm, da_lambda, da_subln, mb_conv_w, mb_conv_b, mb_dt_bias, mb_A_log, mb_D, mb_norm, w_br, w_o, g_ffn2, w_ffn2_gate, w_ffn2_up, w_ffn2_down):
    bp, lp, d = x_prompt.shape
    bs, ls, _ = x_sample.shape
    depth = g_ffn1.shape[0]
    na = state_rwkv_shift.shape[-1]
    h_a, hd_a = state_rwkv_wkv.shape[2], state_rwkv_wkv.shape[3]
    c_a = h_a * hd_a
    lw, la, lg = rw_w_up.shape[1], rw_a_up.shape[1], rw_g_up.shape[1]
    h_b, cqk_h = cache_k.shape[3], cache_k.shape[4]
    vd_b = cache_v.shape[4]
    dh = cqk_h // 2
    c_qk, c_b = h_b * cqk_h, h_b * vd_b
    h_c, p_c, n_c = state_ssm.shape[2], state_ssm.shape[3], state_ssm.shape[4]
    c_c = h_c * p_c
    conv_w, conv_dim = mb_conv_w.shape[1], mb_conv_w.shape[2]
    g_c = (conv_dim - c_c) // (2 * n_c)
    e_c = h_c // g_c
    past = page_table.shape[1] * cache_k.shape[2]
    assert ls == 1, "sample group: one new token per sequence"
    assert hd_a * 2 == LANES and cqk_h == LANES and vd_b == LANES and p_c * 2 == LANES and n_c == LANES
    assert e_c % 2 == 0 and h_c <= LANES and conv_w <= 8
    assert na == 3 * c_a + lw + la + lg

    bf = lambda w: w.astype(BF16)
    nap = _round_up(na, LANES)
    seg_sizes = [na, c_qk, c_qk, c_b, c_c, conv_dim, h_c, 3 * d]
    offs = [0]
    for s in seg_sizes:
        offs.append(offs[-1] + s)
    assert offs[-1] == w_in.shape[-1]
    w_t = jnp.swapaxes(w_in, 1, 2)
    win = [(offs[i], _round_up(seg_sizes[i], LANES)) for i in range(len(seg_sizes))]
    assert win[-1][0] + win[-1][1] <= w_t.shape[1]
    wg1, wu1, wd1 = bf(w_ffn1_gate), bf(w_ffn1_up), bf(w_ffn1_down)
    wg2, wu2, wd2 = bf(w_ffn2_gate), bf(w_ffn2_up), bf(w_ffn2_down)
    w_br3 = bf(w_br).reshape(depth, 3, c_a, d)
    w_o_b = bf(w_o)
    tail_w = nap - 3 * c_a
    w_lora = jnp.zeros((depth, tail_w, 3 * c_a), BF16)
    w_lora = w_lora.at[:, :lw, :c_a].set(bf(rw_w_up))
    w_lora = w_lora.at[:, lw:lw + la, c_a:2 * c_a].set(bf(rw_a_up))
    w_lora = w_lora.at[:, lw + la:lw + la + lg, 2 * c_a:].set(bf(rw_g_up))

    row3 = lambda x: x.reshape(depth, 1, -1)
    g1, gm, g2 = row3(g_ffn1), row3(g_mix), row3(g_ffn2)
    mu = row3(_pad_last(rw_mu, nap))
    rw_par = jnp.stack([rw_w0, rw_a0, rw_k_k, rw_k_a, rw_r_k.reshape(depth, c_a), rw_ln_w, rw_ln_b,
                        jnp.zeros_like(rw_w0)], axis=1)
    qn = row3(da_q_norm)
    kn = row3(da_k_norm)
    sub = row3(da_subln)
    mb_par = jnp.stack([_pad_last(mb_dt_bias, LANES), _pad_last(mb_A_log, LANES), _pad_last(mb_D, LANES)]
                       + [jnp.zeros((depth, LANES), F32)] * 5, axis=1)
    mb_nrm = row3(mb_norm)
    conv_b = row3(mb_conv_b)

    half = dh // 2
    inv = ROPE_THETA ** (-jnp.arange(half, dtype=F32) / half)

    def rope_tables(pos):
        ang = pos.astype(F32)[:, None] * inv[None, :]
        cos, sin = jnp.cos(ang), jnp.sin(ang)
        return jnp.tile(cos, (1, LANES // half)), jnp.tile(jnp.concatenate([-sin, sin], axis=1), (1, LANES // dh))

    cos_p, sin_p = rope_tables(jnp.arange(lp))
    ms_rows = _round_up(bs * ls, BF16_ROWS)
    cos_s, sin_s = rope_tables(jnp.full((ms_rows,), past, jnp.int32))

    xp = x_prompt.reshape(bp * lp, d)
    xs = _pad_rows(x_sample.reshape(bs * ls, d), ms_rows)
    t_a, t_c = RWKV_CHUNK, SSD_CHUNK_T
    outs_p = [[] for _ in range(6)]
    outs_s = [[] for _ in range(6)]

    def project(x, l):
        h = _rmsnorm(x, gm, l)
        ua = _proj(h, w_t, l, *win[0])
        uq = _proj(h, w_t, l, *win[1])
        uk = _proj(h, w_t, l, *win[2])
        uv, uv_b = _proj(h, w_t, l, *win[3], bf16_copy=True)
        uz = _proj(h, w_t, l, *win[4])
        uxbc = _proj(h, w_t, l, *win[5])
        udt = _proj(h, w_t, l, *win[6])
        ug = _proj(h, w_t, l, *win[7])
        return ua, uq, uk, uv, uv_b, uz, uxbc, udt, ug

    def finish(x, o_a, o_b, o_c, ug, l):
        merged = _merge(o_a, o_b, o_c, w_br3, ug, l)
        x = _mm(merged, w_o_b, l, residual=x)
        return _ffn(x, g2, wg2, wu2, wd2, l)

    for l in range(depth):
        lam_init = 0.8 - 0.6 * math.exp(-0.3 * l)

        xp = _ffn(xp, g1, wg1, wu1, wd1, l)
        ua, uq, uk, uv, uv_b, uz, uxbc, udt, ug = project(xp, l)
        ua3 = ua.reshape(bp, lp, nap)
        rkv, lora = _rwkv_prep(ua3, jnp.zeros((bp, 1, nap), F32), mu, w_lora, l, 3 * c_a, lw, la)
        o_a, s_out = _rwkv_scan(rkv, lora, rw_par, jnp.zeros((bp, c_a // LANES, LANES, LANES), F32),
                                l, bp, lp, lp)
        q_r, k_r, k_rb = _qk_prep(uq, uk, cos_p, sin_p, qn, kn, l, dh)
        o_b = _attn_prompt(q_r, k_rb, uv_b, da_lambda, sub, l, bp, lp, lam_init, dh)
        xbc_act = _conv(uxbc.reshape(bp, lp, conv_dim), jnp.zeros((bp, 8, conv_dim), F32), mb_conv_w, conv_b, l)
        o_c, h_out = _ssd(xbc_act, uz, udt, mb_par, mb_nrm, jnp.zeros((bp, h_c // 2, LANES, n_c), F32),
                          l, bp, lp, lp, g_c, e_c, p_c, n_c)
        xp = finish(xp, o_a, o_b, o_c, ug, l)
        outs_p[0].append(k_r.reshape(bp, lp, h_b, cqk_h))
        outs_p[1].append(uv.reshape(bp, lp, h_b, vd_b))
        outs_p[2].append(ua3[:, lp - 1, :na])
        outs_p[3].append(_blockdiag_to_wkv(s_out, hd_a, hd_a))
        outs_p[4].append(uxbc.reshape(bp, lp, conv_dim)[:, lp - (conv_w - 1):])
        outs_p[5].append(h_out.reshape(bp, h_c, p_c, n_c))

        xs = _ffn(xs, g1, wg1, wu1, wd1, l)
        ua, uq, uk, uv, uv_b, uz, uxbc, udt, ug = project(xs, l)
        ua3 = _first_token_rows(ua, bs, t_a)
        shift0 = _pad_last(state_rwkv_shift[l], nap)[:, None, :]
        rkv, lora = _rwkv_prep(ua3, shift0, mu, w_lora, l, 3 * c_a, lw, la)
        o_a_seq, s_out = _rwkv_scan(rkv, lora, rw_par, _wkv_to_blockdiag(state_rwkv_wkv[l]), l, bs, t_a, ls)
        o_a = _pad_rows(o_a_seq.reshape(bs, t_a, c_a)[:, 0], ms_rows)
        q_r, k_r, _ = _qk_prep(uq, uk, cos_s, sin_s, qn, kn, l, dh)
        o_b3 = _attn_sample(page_table, q_r[:bs].reshape(bs, h_b, cqk_h), cache_k, cache_v,
                            k_r[:bs].reshape(bs, h_b, cqk_h), uv[:bs].reshape(bs, h_b, vd_b),
                            da_lambda, sub, l, lam_init, dh)
        o_b = _pad_rows(o_b3.reshape(bs, c_b), ms_rows).astype(BF16)
        conv0 = state_conv[l]
        conv0_8 = jnp.pad(conv0, ((0, 0), (8 - (conv_w - 1), 0), (0, 0)))
        xbc_act = _conv(_first_token_rows(uxbc, bs, t_c), conv0_8, mb_conv_w, conv_b, l)
        h0 = state_ssm[l].reshape(bs, h_c // 2, LANES, n_c)
        o_c_seq, h_out = _ssd(xbc_act, _first_token_rows(uz, bs, t_c).reshape(bs * t_c, c_c),
                              _first_token_rows(udt, bs, t_c).reshape(bs * t_c, LANES),
                              mb_par, mb_nrm, h0, l, bs, t_c, ls, g_c, e_c, p_c, n_c)
        o_c = _pad_rows(o_c_seq.reshape(bs, t_c, c_c)[:, 0], ms_rows)
        xs = finish(xs, o_a, o_b, o_c, ug, l)
        outs_s[0].append(k_r[:bs].reshape(bs, ls, h_b, cqk_h))
        outs_s[1].append(uv[:bs].reshape(bs, ls, h_b, vd_b))
        outs_s[2].append(ua[:bs, :na])
        outs_s[3].append(_blockdiag_to_wkv(s_out, hd_a, hd_a))
        outs_s[4].append(jnp.concatenate([conv0, uxbc[:bs, None, :]], axis=1)[:, ls:])
        outs_s[5].append(h_out.reshape(bs, h_c, p_c, n_c))

    yp = xp.reshape(bp, lp, d)
    ys = xs[:bs * ls].reshape(bs, ls, d)
    return (yp, ys) + tuple(jnp.stack(o) for o in outs_p) + tuple(jnp.stack(o) for o in outs_s)
```

```python
import functools
import math

import jax
import jax.numpy as jnp
from jax import lax
from jax.experimental import pallas as pl
from jax.experimental.pallas import tpu as pltpu

F32 = jnp.float32
BF16 = jnp.bfloat16

NORM_EPS = 1e-6
ROPE_THETA = 10000.0
LN_X_EPS = 64e-5
MB_NORM_EPS = 1e-5

LANES = 128
BF16_ROWS = 16
MIB = 1024 * 1024
VMEM_BIG = 56 * MIB
VMEM_SMALL = 40 * MIB
VMEM_PROJ = 60 * MIB

RWKV_CHAINS = 16
SAMPLE_PAGES_PER_STEP = 8
RWKV_CHUNK = 64
SSD_CHUNK_T = 128
FFN_TF = 256


def _cparams(sem, vmem):
    return pltpu.CompilerParams(dimension_semantics=sem, vmem_limit_bytes=vmem)


def _round_up(n, m):
    return (n + m - 1) // m * m


def _pick_tile(n, cands):
    for c in cands:
        if n % c == 0:
            return c
    raise ValueError(f"no tile for {n} in {cands}")


def _dot(a, b):
    return jnp.dot(a, b, preferred_element_type=F32)


def _dot_nt(a, b):
    return lax.dot_general(a, b, (((1,), (1,)), ((), ())), preferred_element_type=F32)


def _softplus(x):
    return jnp.maximum(x, 0.0) + jnp.log1p(jnp.exp(-jnp.abs(x)))


def _silu(x):
    return x * jax.nn.sigmoid(x)


def _cumsum_rows(x):
    row = lax.broadcasted_iota(jnp.int32, x.shape, 0)
    shift = 1
    while shift < x.shape[0]:
        x = x + jnp.where(row >= shift, pltpu.roll(x, shift, axis=0), 0.0)
        shift *= 2
    return x


def _split_bf16(x):
    hi = x.astype(BF16)
    lo = (x - hi.astype(F32)).astype(BF16)
    return hi, lo


def _rmsnorm_body(x_ref, g_ref, o_ref):
    x = x_ref[...]
    ms = jnp.mean(x * x, axis=-1, keepdims=True)
    o_ref[...] = (x * lax.rsqrt(ms + NORM_EPS) * g_ref[...]).astype(BF16)


def _rmsnorm(x, g, l):
    m, d = x.shape
    tm = _pick_tile(m, (256, 128, 64, 32, 16))
    return pl.pallas_call(
        _rmsnorm_body,
        out_shape=jax.ShapeDtypeStruct((m, d), BF16),
        grid=(m // tm,),
        in_specs=[pl.BlockSpec((tm, d), lambda i: (i, 0)),
                  pl.BlockSpec((None, 1, d), lambda i: (l, 0, 0))],
        out_specs=pl.BlockSpec((tm, d), lambda i: (i, 0)),
        compiler_params=_cparams(("arbitrary",), VMEM_SMALL),
        name="rmsnorm",
    )(x, g)


def _ffn_body(x_ref, g_ref, wg_ref, wu_ref, wd_ref, o_ref, h_scr):
    @pl.when(pl.program_id(1) == 0)
    def _():
        x = x_ref[...]
        ms = jnp.mean(x * x, axis=-1, keepdims=True)
        h_scr[...] = (x * lax.rsqrt(ms + NORM_EPS) * g_ref[...]).astype(BF16)
        o_ref[...] = x

    h = h_scr[...]
    gate = _dot(h, wg_ref[...])
    up = _dot(h, wu_ref[...])
    act = (0.5 * _silu(gate) * up).astype(BF16)
    o_ref[...] += _dot(act, wd_ref[...])


def _ffn(x, g, wg, wu, wd, l):
    m, d = x.shape
    f = wg.shape[-1]
    tm = _pick_tile(m, (512, 256, 128, 64, 32, 16))
    tf = FFN_TF
    return pl.pallas_call(
        _ffn_body,
        out_shape=jax.ShapeDtypeStruct((m, d), F32),
        grid=(m // tm, f // tf),
        in_specs=[pl.BlockSpec((tm, d), lambda i, j: (i, 0), pipeline_mode=pl.Buffered(1)),
                  pl.BlockSpec((None, 1, d), lambda i, j: (l, 0, 0)),
                  pl.BlockSpec((None, d, tf), lambda i, j: (l, 0, j)),
                  pl.BlockSpec((None, d, tf), lambda i, j: (l, 0, j)),
                  pl.BlockSpec((None, tf, d), lambda i, j: (l, j, 0))],
        out_specs=pl.BlockSpec((tm, d), lambda i, j: (i, 0)),
        scratch_shapes=[pltpu.VMEM((tm, d), BF16)],
        compiler_params=_cparams(("arbitrary", "arbitrary"), VMEM_BIG),
        name="ffn",
    )(x, g, wg, wu, wd)


def _mm_body(x_ref, w_ref, *rest, residual):
    acc = _dot(x_ref[...], w_ref[...])
    if residual:
        acc = rest[0][...] + acc
        rest = rest[1:]
    rest[0][...] = acc


def _mm(x, w, l, residual=None):
    m, k = x.shape
    n = w.shape[-1]
    tm = _pick_tile(m, (1024, 512, 256, 128, 64, 32, 16))
    tn = _pick_tile(n, (1024, 512, 256, 128))
    in_specs = [pl.BlockSpec((tm, k), lambda i, j: (i, 0)),
                pl.BlockSpec((None, k, tn), lambda i, j: (l, 0, j))]
    args = [x, w]
    if residual is not None:
        in_specs.append(pl.BlockSpec((tm, tn), lambda i, j: (i, j)))
        args.append(residual)
    return pl.pallas_call(
        functools.partial(_mm_body, residual=residual is not None),
        out_shape=jax.ShapeDtypeStruct((m, n), F32),
        grid=(m // tm, n // tn),
        in_specs=in_specs,
        out_specs=pl.BlockSpec((tm, tn), lambda i, j: (i, j)),
        compiler_params=_cparams(("arbitrary", "arbitrary"), VMEM_BIG),
        name="matmul",
    )(*args)


def _proj_body(x_ref, w_ref, o_ref, *rest):
    acc = _dot_nt(x_ref[...], w_ref[0].astype(BF16))
    o_ref[...] = acc
    if rest:
        rest[0][...] = acc.astype(BF16)


def _proj(x, w_t, l, start, n, bf16_copy=False):
    m, k = x.shape
    assert start % BF16_ROWS == 0 and start + n <= w_t.shape[1]
    tm = _pick_tile(m, (2048, 1024, 512, 256, 128, 64, 32, 16))
    tn = _pick_tile(n, (256, 128))
    out_shape = [jax.ShapeDtypeStruct((m, n), F32)]
    out_specs = [pl.BlockSpec((tm, tn), lambda i, j: (i, j))]
    if bf16_copy:
        out_shape.append(jax.ShapeDtypeStruct((m, n), BF16))
        out_specs.append(pl.BlockSpec((tm, tn), lambda i, j: (i, j)))
    out = pl.pallas_call(
        _proj_body,
        out_shape=out_shape,
        grid=(m // tm, n // tn),
        in_specs=[pl.BlockSpec((tm, k), lambda i, j: (i, 0)),
                  pl.BlockSpec((pl.Element(1), pl.Element(tn), pl.Element(k)),
                               lambda i, j: (l, pl.multiple_of(start + j * tn, BF16_ROWS), 0))],
        out_specs=out_specs,
        compiler_params=_cparams(("arbitrary", "arbitrary"), VMEM_PROJ),
        name="in_proj",
    )(x, w_t)
    return out if bf16_copy else out[0]


def _merge_body(oa_ref, ob_ref, oc_ref, w_ref, g0_ref, g1_ref, g2_ref, o_ref):
    acc = jax.nn.sigmoid(g0_ref[...]) * _dot(oa_ref[...], w_ref[0])
    acc += jax.nn.sigmoid(g1_ref[...]) * _dot(ob_ref[...], w_ref[1])
    acc += jax.nn.sigmoid(g2_ref[...]) * _dot(oc_ref[...], w_ref[2])
    o_ref[...] = acc.astype(BF16)


def _merge(o_a, o_b, o_c, w_br3, ug, l):
    m, c = o_a.shape
    d = w_br3.shape[-1]
    tm = _pick_tile(m, (1024, 512, 256, 128, 64, 32, 16))
    tn = _pick_tile(d, (256, 128))
    nj = d // tn
    o_spec = pl.BlockSpec((tm, c), lambda i, j: (i, 0))
    return pl.pallas_call(
        _merge_body,
        out_shape=jax.ShapeDtypeStruct((m, d), BF16),
        grid=(m // tm, nj),
        in_specs=[o_spec, o_spec, o_spec,
                  pl.BlockSpec((None, 3, c, tn), lambda i, j: (l, 0, 0, j)),
                  pl.BlockSpec((tm, tn), lambda i, j: (i, j)),
                  pl.BlockSpec((tm, tn), lambda i, j: (i, nj + j)),
                  pl.BlockSpec((tm, tn), lambda i, j: (i, 2 * nj + j))],
        out_specs=pl.BlockSpec((tm, tn), lambda i, j: (i, j)),
        compiler_params=_cparams(("arbitrary", "arbitrary"), VMEM_BIG),
        name="merge",
    )(o_a, o_b, o_c, w_br3, ug, ug, ug)


def _rwkv_prep_body(ua_ref, sh_ref, mu_ref, wl_ref, rkv_ref, lora_ref, carry, *, c3, lw, la):
    x = ua_ref[...]
    tm = x.shape[0]

    @pl.when(pl.program_id(1) == 0)
    def _():
        carry[...] = sh_ref[...]

    first_prev = carry[...]
    carry[...] = x[tm - 1:tm, :]
    row = lax.broadcasted_iota(jnp.int32, x.shape, 0)
    prev = jnp.where(row == 0, first_prev, pltpu.roll(x, 1, axis=0))
    us = x + mu_ref[...] * (prev - x)
    rkv_ref[...] = us[:, :c3]
    tail = us[:, c3:]
    lane = lax.broadcasted_iota(jnp.int32, tail.shape, 1)
    act = jnp.where(lane < lw, jnp.tanh(tail), jnp.where(lane < lw + la, tail, jax.nn.sigmoid(tail)))
    lora_ref[...] = _dot(act.astype(BF16), wl_ref[...])


def _rwkv_prep(ua3, shift0, mu, wl, l, c3, lw, la):
    b, lseq, nap = ua3.shape
    tm = _pick_tile(lseq, (128, 64))
    tail_w = nap - c3
    nblk = lseq // tm
    out_sds = jax.ShapeDtypeStruct((b * lseq, c3), F32)
    return pl.pallas_call(
        functools.partial(_rwkv_prep_body, c3=c3, lw=lw, la=la),
        out_shape=(out_sds, out_sds),
        grid=(b, nblk),
        in_specs=[pl.BlockSpec((None, tm, nap), lambda bi, i: (bi, i, 0)),
                  pl.BlockSpec((None, 1, nap), lambda bi, i: (bi, 0, 0)),
                  pl.BlockSpec((None, 1, nap), lambda bi, i: (l, 0, 0)),
                  pl.BlockSpec((None, tail_w, c3), lambda bi, i: (l, 0, 0))],
        out_specs=(pl.BlockSpec((tm, c3), lambda bi, i: (bi * nblk + i, 0)),
                   pl.BlockSpec((tm, c3), lambda bi, i: (bi * nblk + i, 0))),
        scratch_shapes=[pltpu.VMEM((1, nap), F32)],
        compiler_params=_cparams(("arbitrary", "arbitrary"), VMEM_BIG),
        name="rwkv_prep",
    )(ua3, shift0, mu, wl)


def _rwkv_scan_body(r_ref, k_ref, v_ref, wl_ref, al_ref, g_ref, par_ref, s0_ref, o_ref, so_ref, s_scr,
                    *, t, valid, nchunks, nseq, npair):
    c = pl.program_id(2)

    @pl.when(c == 0)
    def _():
        s_scr[...] = s0_ref[...]

    hd = LANES // 2
    lane = lax.broadcasted_iota(jnp.int32, (t, LANES), 1)
    m0 = lane < hd
    ri = lax.broadcasted_iota(jnp.int32, (2 * t, 2 * t), 0)
    ci = lax.broadcasted_iota(jnp.int32, (2 * t, 2 * t), 1)
    same_head = (ri // t) == (ci // t)
    strict = jnp.logical_and(same_head, ci < ri)
    incl = jnp.logical_and(same_head, ci <= ri)
    eye = (ci == ri).astype(F32)
    bi = lax.broadcasted_iota(jnp.int32, (LANES, LANES), 0) // hd
    bj = lax.broadcasted_iota(jnp.int32, (LANES, LANES), 1) // hd
    head_diag = bi == bj
    bf = lambda x: x.astype(BF16)
    twice = lambda x: jnp.concatenate([x, x], axis=0)

    def seg_sum(x):
        s_lo = jnp.sum(jnp.where(m0, x, 0.0), axis=1, keepdims=True)
        s_hi = jnp.sum(jnp.where(m0, 0.0, x), axis=1, keepdims=True)
        return jnp.where(m0, s_lo, s_hi)

    chains = [(i, p) for i in range(nseq) for p in range(npair)]
    blk = lambda ref, i, p: ref[i, :, p * LANES:(p + 1) * LANES]

    def stage(f, *cols):
        return [f(*a) for a in zip(*cols)]

    def elementwise(i, p):
        par = par_ref[:, p * LANES:(p + 1) * LANES]
        w0, a0, kk_w, ka_w = (par[j:j + 1] for j in range(4))
        k = blk(k_ref, i, p)
        v = blk(v_ref, i, p)
        w = -_softplus(-(w0 + blk(wl_ref, i, p))) - 0.5
        logd = -jnp.exp(w)
        a = jax.nn.sigmoid(a0 + blk(al_ref, i, p))
        kkr = k * kk_w
        kk = kkr / jnp.maximum(jnp.sqrt(seg_sum(kkr * kkr)), 1e-12)
        k2 = k * (1.0 + (a - 1.0) * ka_w)
        if valid < t * nchunks:
            ok = c * t + lax.broadcasted_iota(jnp.int32, (t, LANES), 0) < valid
            logd = jnp.where(ok, logd, 0.0)
            kk = jnp.where(ok, kk, 0.0)
            k2m = jnp.where(ok, k2, 0.0)
            vm = jnp.where(ok, v, 0.0)
        else:
            k2m, vm = k2, v
        cs = _cumsum_rows(logd)
        gam = jnp.exp(cs)
        ginv = jnp.exp(-cs)
        rt = blk(r_ref, i, p) * gam
        at = -kk * jnp.exp(cs - logd)
        bt = bf(kk * a * ginv)
        kt = bf(k2m * ginv)
        return rt, at, bt, kt, vm, k2, gam[t - 1:t, :]

    rts, ats, bts, kts, vms, k2s, g_lasts = zip(*[elementwise(i, p) for i, p in chains])

    grams = stage(lambda at, rt, bt, kt: _dot_nt(
        bf(jnp.concatenate([jnp.where(m0, at, 0.0), jnp.where(m0, 0.0, at),
                            jnp.where(m0, rt, 0.0), jnp.where(m0, 0.0, rt)], axis=0)),
        jnp.concatenate([bt, bt, kt, kt], axis=0)), ats, rts, bts, kts)
    a_aks = stage(lambda g: bf(jnp.where(strict, g[:2 * t, 2 * t:], 0.0)), grams)
    m_rbs = stage(lambda g: bf(jnp.where(incl, g[2 * t:, :2 * t], 0.0)), grams)
    m_rks = stage(lambda g: bf(jnp.where(incl, g[2 * t:, 2 * t:], 0.0)), grams)
    npows = stage(lambda g: jnp.where(strict, g[:2 * t, :2 * t], 0.0), grams)
    invs = stage(lambda x: eye + x, npows)
    for _ in range(int(math.log2(t)) - 1):
        npows = stage(lambda x: _dot(bf(x), bf(x)), npows)
        invs = stage(lambda iv, x: iv + _dot(bf(iv), bf(x)), invs, npows)

    s_prevs = [s_scr[i, p] for i, p in chains]
    s_bs = stage(bf, s_prevs)
    v2s = stage(lambda vm: bf(twice(vm)), vms)
    zss = stage(lambda at, sb: _dot_nt(bf(at), sb), ats, s_bs)
    rss = stage(lambda rt, sb: _dot_nt(bf(rt), sb), rts, s_bs)
    rhss = stage(lambda zs, ak, v2: bf(twice(zs) + _dot(ak, v2)), zss, a_aks, v2s)
    w_sts = stage(lambda iv, x: _dot(bf(iv), x), invs, rhss)
    y_sts = stage(lambda rs, rb, w, rk, v2: twice(rs) + _dot(rb, bf(w)) + _dot(rk, v2), rss, m_rbs, w_sts, m_rks, v2s)
    wmats = stage(lambda w: jnp.where(m0, w[:t], w[t:]), w_sts)
    upds = stage(lambda w, vm, bt, kt: _dot(bf(jnp.concatenate([w, vm], axis=0).T), jnp.concatenate([bt, kt], axis=0)),
                 wmats, vms, bts, kts)
    for (i, p), s_prev, upd, g_last in zip(chains, s_prevs, upds, g_lasts):
        s_scr[i, p] = (s_prev + jnp.where(head_diag, upd, 0.0)) * g_last

    inv_hd = 1.0 / hd
    for (i, p), y_st, k2 in zip(chains, y_sts, k2s):
        par = par_ref[:, p * LANES:(p + 1) * LANES]
        rk_w, ln_w, ln_b = (par[j:j + 1] for j in range(4, 7))
        y = jnp.where(m0, y_st[:t], y_st[t:])
        yc = y - seg_sum(y) * inv_hd
        yn = yc * lax.rsqrt(seg_sum(yc * yc) * inv_hd + LN_X_EPS) * ln_w + ln_b
        bonus = seg_sum(blk(r_ref, i, p) * k2 * rk_w) * blk(v_ref, i, p)
        o_ref[i, :, p * LANES:(p + 1) * LANES] = ((yn + bonus) * blk(g_ref, i, p)).astype(BF16)

    @pl.when(c == nchunks - 1)
    def _():
        so_ref[...] = s_scr[...]


def _rwkv_scan(rkv, lora, par, s0, l, b, lseq, valid):
    c = rkv.shape[1] // 3
    t = RWKV_CHUNK
    nchunks = lseq // t
    nseq = _pick_tile(b, (4, 2, 1))
    npair = _pick_tile(c // LANES, (RWKV_CHAINS // nseq, 1))
    ngrp = c // (npair * LANES)
    rkv3 = rkv.reshape(b, lseq, 3 * c)
    lora3 = lora.reshape(b, lseq, 3 * c)

    def col(off):
        return pl.BlockSpec((nseq, t, npair * LANES), lambda bi, p, ci: (bi, ci, off * ngrp + p))

    state = pl.BlockSpec((nseq, npair, LANES, LANES), lambda bi, p, ci: (bi, p, 0, 0))
    o_a, s_out = pl.pallas_call(
        functools.partial(_rwkv_scan_body, t=t, valid=valid, nchunks=nchunks, nseq=nseq, npair=npair),
        out_shape=(jax.ShapeDtypeStruct((b, lseq, c), BF16),
                   jax.ShapeDtypeStruct((b, c // LANES, LANES, LANES), F32)),
        grid=(b // nseq, ngrp, nchunks),
        in_specs=[col(0), col(1), col(2), col(0), col(1), col(2),
                  pl.BlockSpec((None, 8, npair * LANES), lambda bi, p, ci: (l, 0, p)),
                  state],
        out_specs=(pl.BlockSpec((nseq, t, npair * LANES), lambda bi, p, ci: (bi, ci, p)), state),
        scratch_shapes=[pltpu.VMEM((nseq, npair, LANES, LANES), F32)],
        compiler_params=_cparams(("arbitrary", "arbitrary", "arbitrary"), VMEM_SMALL),
        name="rwkv_scan",
    )(rkv3, rkv3, rkv3, lora3, lora3, lora3, par, s0)
    return o_a.reshape(b * lseq, c), s_out


def _qk_prep_body(uq_ref, uk_ref, cos_ref, sin_ref, qn_ref, kn_ref, q_ref, k_ref, kb_ref, *, dh, scale):
    cos = cos_ref[...]
    sin = sin_ref[...]
    shape = cos.shape
    lane = lax.broadcasted_iota(jnp.int32, shape, 1)
    first_half = (lane % dh) < (dh // 2)
    si = lax.broadcasted_iota(jnp.int32, (LANES, LANES), 0) // dh
    sj = lax.broadcasted_iota(jnp.int32, (LANES, LANES), 1) // dh
    seg = (si == sj).astype(BF16)

    def prep(x, w):
        hi, lo = _split_bf16(x * x)
        ss = _dot(hi, seg) + _dot(lo, seg)
        y = x * lax.rsqrt(ss * (1.0 / dh) + NORM_EPS) * w
        partner = jnp.where(first_half, pltpu.roll(y, LANES - dh // 2, axis=1), pltpu.roll(y, dh // 2, axis=1))
        return y * cos + partner * sin

    q_ref[...] = prep(uq_ref[...], qn_ref[...]) * scale
    kr = prep(uk_ref[...], kn_ref[...])
    k_ref[...] = kr
    kb_ref[...] = kr.astype(BF16)


def _qk_prep(uq, uk, cos, sin, qn, kn, l, dh):
    m, cq = uq.shape
    tm = cos.shape[0]
    nh = cq // LANES
    nrep = m // tm
    blk = pl.BlockSpec((tm, LANES), lambda i, h: (i, h))
    tab = pl.BlockSpec((tm, LANES), lambda i, h: (0, 0))
    nrm = pl.BlockSpec((None, 1, LANES), lambda i, h: (l, 0, 0))
    return pl.pallas_call(
        functools.partial(_qk_prep_body, dh=dh, scale=dh ** -0.5),
        out_shape=(jax.ShapeDtypeStruct((m, cq), F32), jax.ShapeDtypeStruct((m, cq), F32),
                   jax.ShapeDtypeStruct((m, cq), BF16)),
        grid=(nrep, nh),
        in_specs=[blk, blk, tab, tab, nrm, nrm],
        out_specs=(blk, blk, blk),
        compiler_params=_cparams(("arbitrary", "arbitrary"), VMEM_SMALL),
        name="qk_prep",
    )(uq, uk, cos, sin, qn, kn)


def _diff_lambda(lp, lam_init):
    s1 = jnp.sum(lp[0:1] * lp[1:2], axis=1, keepdims=True)
    s2 = jnp.sum(lp[2:3] * lp[3:4], axis=1, keepdims=True)
    return jnp.exp(s1) - jnp.exp(s2) + lam_init


def _sub_ln(o, sub, lam_init):
    ms = jnp.mean(o * o, axis=-1, keepdims=True)
    return o * lax.rsqrt(ms + NORM_EPS) * sub * (1.0 - lam_init)


def _attn_prompt_body(q_ref, k_ref, v_ref, lam_ref, sub_ref, o_ref, *, lam_init, dh, tq):
    lseq = q_ref.shape[0]
    lane = lax.broadcasted_iota(jnp.int32, (tq, LANES), 1)
    tri = lax.broadcasted_iota(jnp.int32, (tq, tq), 1) <= lax.broadcasted_iota(jnp.int32, (tq, tq), 0)
    lam = _diff_lambda(lam_ref[...], lam_init)
    sub = sub_ref[...]
    for qi in range(lseq // tq):
        lo, hi = qi * tq, (qi + 1) * tq
        q = q_ref[lo:hi, :]
        k_diag = k_ref[lo:hi, :]
        v_diag = v_ref[lo:hi, :]
        outs = []
        for qc in (jnp.where(lane < dh, q, 0.0).astype(BF16), jnp.where(lane < dh, 0.0, q).astype(BF16)):
            s_diag = jnp.where(tri, _dot_nt(qc, k_diag), -jnp.inf)
            m = jnp.max(s_diag, axis=1, keepdims=True)
            if qi:
                s_off = _dot_nt(qc, k_ref[0:lo, :])
                m = jnp.maximum(m, jnp.max(s_off, axis=1, keepdims=True))
            p_diag = jnp.exp(s_diag - m)
            l = jnp.sum(p_diag, axis=1, keepdims=True)
            o = _dot(p_diag.astype(BF16), v_diag)
            if qi:
                p_off = jnp.exp(s_off - m)
                l = l + jnp.sum(p_off, axis=1, keepdims=True)
                o = o + _dot(p_off.astype(BF16), v_ref[0:lo, :])
            outs.append(o * (1.0 / l))
        o_ref[lo:hi, :] = _sub_ln(outs[0] - lam * outs[1], sub, lam_init).astype(BF16)


def _attn_prompt(q, kb, vb, lam_p, sub, l, b, lseq, lam_init, dh):
    m, cq = q.shape
    nh = cq // LANES
    tq = _pick_tile(lseq, (256, 128))
    blk = pl.BlockSpec((lseq, LANES), lambda bi, h: (bi, h))
    return pl.pallas_call(
        functools.partial(_attn_prompt_body, lam_init=lam_init, dh=dh, tq=tq),
        out_shape=jax.ShapeDtypeStruct((m, cq), BF16),
        grid=(b, nh),
        in_specs=[blk, blk, blk,
                  pl.BlockSpec((None,) + lam_p.shape[1:], lambda bi, h: (l, 0, 0)),
                  pl.BlockSpec((None, 1, LANES), lambda bi, h: (l, 0, 0))],
        out_specs=blk,
        compiler_params=_cparams(("arbitrary", "arbitrary"), VMEM_SMALL),
        name="attn_prompt",
    )(q, kb, vb, lam_p, sub)


def _attn_sample_body(pt_ref, q_ref, *rest, lam_init, dh, nsteps, npp):
    del pt_ref
    kc_refs, vc_refs = rest[:npp], rest[npp:2 * npp]
    kcur_ref, vcur_ref, lam_ref, sub_ref, o_ref, m_scr, l_scr, acc_scr = rest[2 * npp:]
    p = pl.program_id(1)
    q = q_ref[...]
    nh = q.shape[0]
    lane = lax.broadcasted_iota(jnp.int32, q.shape, 1)
    q2 = jnp.concatenate([jnp.where(lane < dh, q, 0.0), jnp.where(lane < dh, 0.0, q)], axis=0)
    twice = lambda x: jnp.concatenate([x, x], axis=0)

    @pl.when(p == 0)
    def _():
        s_cur = jnp.sum(q2 * twice(kcur_ref[...]), axis=1, keepdims=True)
        m_scr[...] = jnp.broadcast_to(s_cur, m_scr.shape)
        l_scr[...] = jnp.ones_like(l_scr)
        acc_scr[...] = twice(vcur_ref[...])

    ps = kc_refs[0].shape[0]
    rows = ps * nh
    rid = lax.broadcasted_iota(jnp.int32, (2 * nh, rows), 0)
    cid = lax.broadcasted_iota(jnp.int32, (2 * nh, rows), 1)
    own_head = (cid % nh) == (rid % nh)
    q2b = q2.astype(BF16)
    ss = [jnp.where(own_head, _dot_nt(q2b, kc[...].reshape(rows, LANES).astype(BF16)), -jnp.inf) for kc in kc_refs]
    m_old = m_scr[:, 0:1]
    m_new = m_old
    for s_j in ss:
        m_new = jnp.maximum(m_new, jnp.max(s_j, axis=1, keepdims=True))
    alpha = jnp.exp(m_old - m_new)
    ps_j = [jnp.exp(s_j - m_new) for s_j in ss]
    l_new = alpha * l_scr[:, 0:1]
    acc = alpha * acc_scr[...]
    for p_j, vc in zip(ps_j, vc_refs):
        l_new = l_new + jnp.sum(p_j, axis=1, keepdims=True)
        acc = acc + _dot(p_j.astype(BF16), vc[...].reshape(rows, LANES).astype(BF16))
    m_scr[...] = jnp.broadcast_to(m_new, m_scr.shape)
    l_scr[...] = jnp.broadcast_to(l_new, l_scr.shape)
    acc_scr[...] = acc

    @pl.when(p == nsteps - 1)
    def _():
        lam = _diff_lambda(lam_ref[...], lam_init)
        o = acc[:nh] * (1.0 / l_new[:nh]) - lam * (acc[nh:] * (1.0 / l_new[nh:]))
        o_ref[...] = _sub_ln(o, sub_ref[...], lam_init)


def _attn_sample(page_table, q3, cache_k, cache_v, kcur3, vcur3, lam_p, sub, l, lam_init, dh):
    bs, nh, _ = q3.shape
    npages = page_table.shape[1]
    ps = cache_k.shape[2]
    npp = _pick_tile(npages, (SAMPLE_PAGES_PER_STEP, 2, 1))
    nsteps = npages // npp
    cur = pl.BlockSpec((None, nh, LANES), lambda bi, p, pt: (bi, 0, 0))

    def page(j):
        return pl.BlockSpec((None, None, ps, nh, LANES), lambda bi, p, pt: (l, pt[bi, p * npp + j], 0, 0, 0))

    pages = [page(j) for j in range(npp)]
    grid_spec = pltpu.PrefetchScalarGridSpec(
        num_scalar_prefetch=1,
        grid=(bs, nsteps),
        in_specs=[cur] + pages + pages + [
            cur, cur,
            pl.BlockSpec((None,) + lam_p.shape[1:], lambda bi, p, pt: (l, 0, 0)),
            pl.BlockSpec((None, 1, LANES), lambda bi, p, pt: (l, 0, 0))],
        out_specs=cur,
        scratch_shapes=[pltpu.VMEM((2 * nh, LANES), F32)] * 3,
    )
    return pl.pallas_call(
        functools.partial(_attn_sample_body, lam_init=lam_init, dh=dh, nsteps=nsteps, npp=npp),
        out_shape=jax.ShapeDtypeStruct((bs, nh, LANES), F32),
        grid_spec=grid_spec,
        compiler_params=_cparams(("arbitrary", "arbitrary"), VMEM_SMALL),
        name="attn_sample",
    )(page_table, q3, *([cache_k] * npp), *([cache_v] * npp), kcur3, vcur3, lam_p, sub)


def _conv_body(x_ref, c0_ref, w_ref, b_ref, o_ref, carry, *, width):
    x = x_ref[...]
    tm = x.shape[0]

    @pl.when(pl.program_id(1) == 0)
    def _():
        carry[...] = c0_ref[...]

    hist = carry[...]
    carry[...] = x[tm - 8:tm, :]
    w = w_ref[...]
    row8 = lax.broadcasted_iota(jnp.int32, hist.shape, 0)
    acc = b_ref[...] + x * w[width - 1:width]
    for j in range(1, width):
        rolled = pltpu.roll(x, j, axis=0)
        head = jnp.where(row8 < j, pltpu.roll(hist, j, axis=0), rolled[0:8])
        xj = jnp.concatenate([head, rolled[8:]], axis=0)
        acc = acc + xj * w[width - 1 - j:width - j]
    o_ref[...] = _silu(acc)


def _conv(x3, conv0_8, w, bias, l):
    b, lseq, cd = x3.shape
    width = w.shape[1]
    tm = _pick_tile(lseq, (256, 128))
    nblk = lseq // tm
    return pl.pallas_call(
        functools.partial(_conv_body, width=width),
        out_shape=jax.ShapeDtypeStruct((b * lseq, cd), F32),
        grid=(b, nblk),
        in_specs=[pl.BlockSpec((None, tm, cd), lambda bi, i: (bi, i, 0)),
                  pl.BlockSpec((None, 8, cd), lambda bi, i: (bi, 0, 0)),
                  pl.BlockSpec((None, width, cd), lambda bi, i: (l, 0, 0)),
                  pl.BlockSpec((None, 1, cd), lambda bi, i: (l, 0, 0))],
        out_specs=pl.BlockSpec((tm, cd), lambda bi, i: (bi * nblk + i, 0)),
        scratch_shapes=[pltpu.VMEM((8, cd), F32)],
        compiler_params=_cparams(("arbitrary", "arbitrary"), VMEM_SMALL),
        name="mamba_conv",
    )(x3, conv0_8, w, bias)


def _ssd_body(xbc_ref, z_ref, dt_ref, par_ref, nrm_ref, h0_ref, o_ref, ho_ref, st_scr,
              *, t, valid, nchunks, n_groups, heads_per_group, p_dim, n_dim):
    c = pl.program_id(1)

    @pl.when(c == 0)
    def _():
        st_scr[...] = h0_ref[...]

    par = par_ref[...]
    dt = _softplus(dt_ref[...] + par[0:1])
    if valid < t * nchunks:
        ok = c * t + lax.broadcasted_iota(jnp.int32, dt.shape, 0) < valid
        dt = jnp.where(ok, dt, 0.0)
    a_neg = -jnp.exp(par[1:2])
    d_skip = par[2:3]
    ri = lax.broadcasted_iota(jnp.int32, (t, t), 0)
    ci = lax.broadcasted_iota(jnp.int32, (t, t), 1)
    tril = ci <= ri
    acs = _cumsum_rows(dt * a_neg)
    acs_t = acs.T
    tot = acs[t - 1:t, :]
    lane = lax.broadcasted_iota(jnp.int32, (t, LANES), 1)
    m0 = lane < p_dim
    m0r = lax.broadcasted_iota(jnp.int32, (1, LANES), 1) < p_dim
    prow = lax.broadcasted_iota(jnp.int32, (LANES, n_dim), 0) < p_dim
    n_heads = n_groups * heads_per_group
    xs_w = n_heads * p_dim
    pairs_per_group = heads_per_group // 2
    gw = heads_per_group * p_dim

    for g in range(n_groups):
        b_g = xbc_ref[:, xs_w + g * n_dim: xs_w + (g + 1) * n_dim].astype(BF16)
        c_off = xs_w + n_groups * n_dim
        c_g = xbc_ref[:, c_off + g * n_dim: c_off + (g + 1) * n_dim].astype(BF16)
        cb = _dot_nt(c_g, b_g)
        ys = []
        for pp in range(pairs_per_group):
            pr = g * pairs_per_group + pp
            h_lo, h_hi = 2 * pr, 2 * pr + 1
            xs = xbc_ref[:, pr * LANES:(pr + 1) * LANES]
            col_lo, col_hi = acs[:, h_lo:h_lo + 1], acs[:, h_hi:h_hi + 1]
            l_lo = jnp.exp(jnp.where(tril, col_lo - acs_t[h_lo:h_lo + 1, :], -1e30))
            l_hi = jnp.exp(jnp.where(tril, col_hi - acs_t[h_hi:h_hi + 1, :], -1e30))
            x_dt = xs * jnp.where(m0, dt[:, h_lo:h_lo + 1], dt[:, h_hi:h_hi + 1])
            x_b = x_dt.astype(BF16)
            y_diag = jnp.where(m0, _dot((cb * l_lo).astype(BF16), x_b), _dot((cb * l_hi).astype(BF16), x_b))
            st = st_scr[pr]
            colp = jnp.where(m0, col_lo, col_hi)
            y_off = jnp.exp(colp) * _dot_nt(c_g, st.astype(BF16))
            totp = jnp.where(m0r, tot[:, h_lo:h_lo + 1], tot[:, h_hi:h_hi + 1])
            contrib = _dot((x_dt * jnp.exp(totp - colp)).T.astype(BF16), b_g)
            chunk_decay = jnp.where(prow, jnp.exp(tot[:, h_lo:h_lo + 1]), jnp.exp(tot[:, h_hi:h_hi + 1]))
            st_scr[pr] = chunk_decay * st + contrib
            dp = jnp.where(m0r, d_skip[:, h_lo:h_lo + 1], d_skip[:, h_hi:h_hi + 1])
            ys.append(y_diag + y_off + dp * xs)
        yg = ys[0] if len(ys) == 1 else jnp.concatenate(ys, axis=1)
        yz = yg * _silu(z_ref[:, g * gw:(g + 1) * gw])
        ms = jnp.mean(yz * yz, axis=-1, keepdims=True)
        o_ref[:, g * gw:(g + 1) * gw] = (yz * lax.rsqrt(ms + MB_NORM_EPS) * nrm_ref[:, g * gw:(g + 1) * gw]).astype(BF16)

    @pl.when(c == nchunks - 1)
    def _():
        ho_ref[...] = st_scr[...]


def _ssd(xbc, uz, udt, par, nrm, h0, l, b, lseq, valid, n_groups, heads_per_group, p_dim, n_dim):
    cd = xbc.shape[1]
    cc = uz.shape[1]
    t = SSD_CHUNK_T
    nchunks = lseq // t
    npair = h0.shape[1]
    return pl.pallas_call(
        functools.partial(_ssd_body, t=t, valid=valid, nchunks=nchunks, n_groups=n_groups,
                          heads_per_group=heads_per_group, p_dim=p_dim, n_dim=n_dim),
        out_shape=(jax.ShapeDtypeStruct((b * lseq, cc), BF16),
                   jax.ShapeDtypeStruct(h0.shape, F32)),
        grid=(b, nchunks),
        in_specs=[pl.BlockSpec((t, cd), lambda bi, ci: (bi * nchunks + ci, 0)),
                  pl.BlockSpec((t, cc), lambda bi, ci: (bi * nchunks + ci, 0)),
                  pl.BlockSpec((t, LANES), lambda bi, ci: (bi * nchunks + ci, 0)),
                  pl.BlockSpec((None, 8, LANES), lambda bi, ci: (l, 0, 0)),
                  pl.BlockSpec((None, 1, cc), lambda bi, ci: (l, 0, 0)),
                  pl.BlockSpec((None, npair, LANES, n_dim), lambda bi, ci: (bi, 0, 0, 0))],
        out_specs=(pl.BlockSpec((t, cc), lambda bi, ci: (bi * nchunks + ci, 0)),
                   pl.BlockSpec((None, npair, LANES, n_dim), lambda bi, ci: (bi, 0, 0, 0))),
        scratch_shapes=[pltpu.VMEM((npair, LANES, n_dim), F32)],
        compiler_params=_cparams(("arbitrary", "arbitrary"), VMEM_SMALL),
        name="ssd",
    )(xbc, uz, udt, par, nrm, h0)


def _pad_last(x, n):
    return jnp.pad(x, [(0, 0)] * (x.ndim - 1) + [(0, n - x.shape[-1])])


def _pad_rows(x, n):
    return jnp.pad(x, [(0, n - x.shape[0])] + [(0, 0)] * (x.ndim - 1))


def _wkv_to_blockdiag(wkv):
    b, h, hv, hk = wkv.shape
    w = wkv.reshape(b, h // 2, 2, hv, hk)
    z = jnp.zeros_like(w[:, :, 0])
    top = jnp.concatenate([w[:, :, 0], z], axis=-1)
    bot = jnp.concatenate([z, w[:, :, 1]], axis=-1)
    return jnp.concatenate([top, bot], axis=-2)


def _blockdiag_to_wkv(s, hv, hk):
    b, p = s.shape[:2]
    return jnp.stack([s[:, :, :hv, :hk], s[:, :, hv:, hk:]], axis=2).reshape(b, 2 * p, hv, hk)


def _first_token_rows(x_rows, bs, t):
    return jnp.pad(x_rows[:bs, None, :], ((0, 0), (0, t - 1), (0, 0)))


def kernel(x_prompt, x_sample, cache_k, cache_v, page_table, state_rwkv_shift, state_rwkv_wkv, state_conv, state_ssm, g_ffn1, w_ffn1_gate, w_ffn1_up, w_ffn1_down, g_mix, w_in, rw_mu, rw_w0, rw_w_up, rw_a0, rw_a_up, rw_g_up, rw_k_k, rw_k_a, rw_r_k, rw_ln_w, rw_ln_b, da_q_norm, da_k_norm, da_lambda, da_subln, mb_conv_w, mb_conv_b, mb_dt_bias, mb_A_log, mb_D, mb_norm, w_br, w_o, g_ffn2, w_ffn2_gate, w_ffn2_up, w_ffn2_down):
    bp, lp, d = x_prompt.shape
    bs, ls, _ = x_sample.shape
    depth = g_ffn1.shape[0]
    na = state_rwkv_shift.shape[-1]
    h_a, hd_a = state_rwkv_wkv.shape[2], state_rwkv_wkv.shape[3]
    c_a = h_a * hd_a
    lw, la, lg = rw_w_up.shape[1], rw_a_up.shape[1], rw_g_up.shape[1]
    h_b, cqk_h = cache_k.shape[3], cache_k.shape[4]
    vd_b = cache_v.shape[4]
    dh = cqk_h // 2
    c_qk, c_b = h_b * cqk_h, h_b * vd_b
    h_c, p_c, n_c = state_ssm.shape[2], state_ssm.shape[3], state_ssm.shape[4]
    c_c = h_c * p_c
    conv_w, conv_dim = mb_conv_w.shape[1], mb_conv_w.shape[2]
    g_c = (conv_dim - c_c) // (2 * n_c)
    e_c = h_c // g_c
    past = page_table.shape[1] * cache_k.shape[2]
    assert ls == 1, "sample group: one new token per sequence"
    assert hd_a * 2 == LANES and cqk_h == LANES and vd_b == LANES and p_c * 2 == LANES and n_c == LANES
    assert e_c % 2 == 0 and h_c <= LANES and conv_w <= 8
    assert na == 3 * c_a + lw + la + lg

    bf = lambda w: w.astype(BF16)
    nap = _round_up(na, LANES)
    seg_sizes = [na, c_qk, c_qk, c_b, c_c, conv_dim, h_c, 3 * d]
    offs = [0]
    for s in seg_sizes:
        offs.append(offs[-1] + s)
    assert offs[-1] == w_in.shape[-1]
    w_t = jnp.swapaxes(w_in, 1, 2)
    win = [(offs[i], _round_up(seg_sizes[i], LANES)) for i in range(len(seg_sizes))]
    assert win[-1][0] + win[-1][1] <= w_t.shape[1]
    wg1, wu1, wd1 = bf(w_ffn1_gate), bf(w_ffn1_up), bf(w_ffn1_down)
    wg2, wu2, wd2 = bf(w_ffn2_gate), bf(w_ffn2_up), bf(w_ffn2_down)
    w_br3 = bf(w_br).reshape(depth, 3, c_a, d)
    w_o_b = bf(w_o)
    tail_w = nap - 3 * c_a
    w_lora = jnp.zeros((depth, tail_w, 3 * c_a), BF16)
    w_lora = w_lora.at[:, :lw, :c_a].set(bf(rw_w_up))
    w_lora = w_lora.at[:, lw:lw + la, c_a:2 * c_a].set(bf(rw_a_up))
    w_lora = w_lora.at[:, lw + la:lw + la + lg, 2 * c_a:].set(bf(rw_g_up))

    row3 = lambda x: x.reshape(depth, 1, -1)
    g1, gm, g2 = row3(g_ffn1), row3(g_mix), row3(g_ffn2)
    mu = row3(_pad_last(rw_mu, nap))
    rw_par = jnp.stack([rw_w0, rw_a0, rw_k_k, rw_k_a, rw_r_k.reshape(depth, c_a), rw_ln_w, rw_ln_b,
                        jnp.zeros_like(rw_w0)], axis=1)
    qn = row3(da_q_norm)
    kn = row3(da_k_norm)
    sub = row3(da_subln)
    mb_par = jnp.stack([_pad_last(mb_dt_bias, LANES), _pad_last(mb_A_log, LANES), _pad_last(mb_D, LANES)]
                       + [jnp.zeros((depth, LANES), F32)] * 5, axis=1)
    mb_nrm = row3(mb_norm)
    conv_b = row3(mb_conv_b)

    half = dh // 2
    inv = ROPE_THETA ** (-jnp.arange(half, dtype=F32) / half)

    def rope_tables(pos):
        ang = pos.astype(F32)[:, None] * inv[None, :]
        cos, sin = jnp.cos(ang), jnp.sin(ang)
        return jnp.tile(cos, (1, LANES // half)), jnp.tile(jnp.concatenate([-sin, sin], axis=1), (1, LANES // dh))

    cos_p, sin_p = rope_tables(jnp.arange(lp))
    ms_rows = _round_up(bs * ls, BF16_ROWS)
    cos_s, sin_s = rope_tables(jnp.full((ms_rows,), past, jnp.int32))

    xp = x_prompt.reshape(bp * lp, d)
    xs = _pad_rows(x_sample.reshape(bs * ls, d), ms_rows)
    t_a, t_c = RWKV_CHUNK, SSD_CHUNK_T
    outs_p = [[] for _ in range(6)]
    outs_s = [[] for _ in range(6)]

    def project(x, l):
        h = _rmsnorm(x, gm, l)
        ua = _proj(h, w_t, l, *win[0])
        uq = _proj(h, w_t, l, *win[1])
        uk = _proj(h, w_t, l, *win[2])
        uv, uv_b = _proj(h, w_t, l, *win[3], bf16_copy=True)
        uz = _proj(h, w_t, l, *win[4])
        uxbc = _proj(h, w_t, l, *win[5])
        udt = _proj(h, w_t, l, *win[6])
        ug = _proj(h, w_t, l, *win[7])
        return ua, uq, uk, uv, uv_b, uz, uxbc, udt, ug

    def finish(x, o_a, o_b, o_c, ug, l):
        merged = _merge(o_a, o_b, o_c, w_br3, ug, l)
        x = _mm(merged, w_o_b, l, residual=x)
        return _ffn(x, g2, wg2, wu2, wd2, l)

    for l in range(depth):
        lam_init = 0.8 - 0.6 * math.exp(-0.3 * l)

        xp = _ffn(xp, g1, wg1, wu1, wd1, l)
        ua, uq, uk, uv, uv_b, uz, uxbc, udt, ug = project(xp, l)
        ua3 = ua.reshape(bp, lp, nap)
        rkv, lora = _rwkv_prep(ua3, jnp.zeros((bp, 1, nap), F32), mu, w_lora, l, 3 * c_a, lw, la)
        o_a, s_out = _rwkv_scan(rkv, lora, rw_par, jnp.zeros((bp, c_a // LANES, LANES, LANES), F32),
                                l, bp, lp, lp)
        q_r, k_r, k_rb = _qk_prep(uq, uk, cos_p, sin_p, qn, kn, l, dh)
        o_b = _attn_prompt(q_r, k_rb, uv_b, da_lambda, sub, l, bp, lp, lam_init, dh)
        xbc_act = _conv(uxbc.reshape(bp, lp, conv_dim), jnp.zeros((bp, 8, conv_dim), F32), mb_conv_w, conv_b, l)
        o_c, h_out = _ssd(xbc_act, uz, udt, mb_par, mb_nrm, jnp.zeros((bp, h_c // 2, LANES, n_c), F32),
                          l, bp, lp, lp, g_c, e_c, p_c, n_c)
        xp = finish(xp, o_a, o_b, o_c, ug, l)
        outs_p[0].append(k_r.reshape(bp, lp, h_b, cqk_h))
        outs_p[1].append(uv.reshape(bp, lp, h_b, vd_b))
        outs_p[2].append(ua3[:, lp - 1, :na])
        outs_p[3].append(_blockdiag_to_wkv(s_out, hd_a, hd_a))
        outs_p[4].append(uxbc.reshape(bp, lp, conv_dim)[:, lp - (conv_w - 1):])
        outs_p[5].append(h_out.reshape(bp, h_c, p_c, n_c))

        xs = _ffn(xs, g1, wg1, wu1, wd1, l)
        ua, uq, uk, uv, uv_b, uz, uxbc, udt, ug = project(xs, l)
        ua3 = _first_token_rows(ua, bs, t_a)
        shift0 = _pad_last(state_rwkv_shift[l], nap)[:, None, :]
        rkv, lora = _rwkv_prep(ua3, shift0, mu, w_lora, l, 3 * c_a, lw, la)
        o_a_seq, s_out = _rwkv_scan(rkv, lora, rw_par, _wkv_to_blockdiag(state_rwkv_wkv[l]), l, bs, t_a, ls)
        o_a = _pad_rows(o_a_seq.reshape(bs, t_a, c_a)[:, 0], ms_rows)
        q_r, k_r, _ = _qk_prep(uq, uk, cos_s, sin_s, qn, kn, l, dh)
        o_b3 = _attn_sample(page_table, q_r[:bs].reshape(bs, h_b, cqk_h), cache_k, cache_v,
                            k_r[:bs].reshape(bs, h_b, cqk_h), uv[:bs].reshape(bs, h_b, vd_b),
                            da_lambda, sub, l, lam_init, dh)
        o_b = _pad_rows(o_b3.reshape(bs, c_b), ms_rows).astype(BF16)
        conv0 = state_conv[l]
        conv0_8 = jnp.pad(conv0, ((0, 0), (8 - (conv_w - 1), 0), (0, 0)))
        xbc_act = _conv(_first_token_rows(uxbc, bs, t_c), conv0_8, mb_conv_w, conv_b, l)
        h0 = state_ssm[l].reshape(bs, h_c // 2, LANES, n_c)
        o_c_seq, h_out = _ssd(xbc_act, _first_token_rows(uz, bs, t_c).reshape(bs * t_c, c_c),
                              _first_token_rows(udt, bs, t_c).reshape(bs * t_c, LANES),
                              mb_par, mb_nrm, h0, l, bs, t_c, ls, g_c, e_c, p_c, n_c)
        o_c = _pad_rows(o_c_seq.reshape(bs, t_c, c_c)[:, 0], ms_rows)
        xs = finish(xs, o_a, o_b, o_c, ug, l)
        outs_s[0].append(k_r[:bs].reshape(bs, ls, h_b, cqk_h))
        outs_s[1].append(uv[:bs].reshape(bs, ls, h_b, vd_b))
        outs_s[2].append(ua[:bs, :na])
        outs_s[3].append(_blockdiag_to_wkv(s_out, hd_a, hd_a))
        outs_s[4].append(jnp.concatenate([conv0, uxbc[:bs, None, :]], axis=1)[:, ls:])
        outs_s[5].append(h_out.reshape(bs, h_c, p_c, n_c))

    yp = xp.reshape(bp, lp, d)
    ys = xs[:bs * ls].reshape(bs, ls, d)
    return (yp, ys) + tuple(jnp.stack(o) for o in outs_p) + tuple(jnp.stack(o) for o in outs_s)
```

```python
import functools
import math

import jax
import jax.numpy as jnp
from jax import lax
from jax.experimental import pallas as pl
from jax.experimental.pallas import tpu as pltpu

F32 = jnp.float32
BF16 = jnp.bfloat16

NORM_EPS = 1e-6
ROPE_THETA = 10000.0
LN_X_EPS = 64e-5
MB_NORM_EPS = 1e-5

LANES = 128
BF16_ROWS = 16
MIB = 1024 * 1024
VMEM_BIG = 56 * MIB
VMEM_SMALL = 40 * MIB
VMEM_PROJ = 60 * MIB

RWKV_CHAINS = 16
SAMPLE_PAGES_PER_STEP = 8
RWKV_CHUNK = 64
SSD_CHUNK_T = 128
FFN_TF = 256


def _cparams(sem, vmem):
    return pltpu.CompilerParams(dimension_semantics=sem, vmem_limit_bytes=vmem)


def _round_up(n, m):
    return (n + m - 1) // m * m


def _pick_tile(n, cands):
    for c in cands:
        if n % c == 0:
            return c
    raise ValueError(f"no tile for {n} in {cands}")


def _dot(a, b):
    return jnp.dot(a, b, preferred_element_type=F32)


def _dot_nt(a, b):
    return lax.dot_general(a, b, (((1,), (1,)), ((), ())), preferred_element_type=F32)


def _softplus(x):
    return jnp.maximum(x, 0.0) + jnp.log1p(jnp.exp(-jnp.abs(x)))


def _silu(x):
    return x * jax.nn.sigmoid(x)


def _cumsum_rows(x):
    row = lax.broadcasted_iota(jnp.int32, x.shape, 0)
    shift = 1
    while shift < x.shape[0]:
        x = x + jnp.where(row >= shift, pltpu.roll(x, shift, axis=0), 0.0)
        shift *= 2
    return x


def _split_bf16(x):
    hi = x.astype(BF16)
    lo = (x - hi.astype(F32)).astype(BF16)
    return hi, lo


def _rmsnorm_body(x_ref, g_ref, o_ref):
    x = x_ref[...]
    ms = jnp.mean(x * x, axis=-1, keepdims=True)
    o_ref[...] = (x * lax.rsqrt(ms + NORM_EPS) * g_ref[...]).astype(BF16)


def _rmsnorm(x, g, l):
    m, d = x.shape
    tm = _pick_tile(m, (256, 128, 64, 32, 16))
    return pl.pallas_call(
        _rmsnorm_body,
        out_shape=jax.ShapeDtypeStruct((m, d), BF16),
        grid=(m // tm,),
        in_specs=[pl.BlockSpec((tm, d), lambda i: (i, 0)),
                  pl.BlockSpec((None, 1, d), lambda i: (l, 0, 0))],
        out_specs=pl.BlockSpec((tm, d), lambda i: (i, 0)),
        compiler_params=_cparams(("arbitrary",), VMEM_SMALL),
        name="rmsnorm",
    )(x, g)


def _ffn_body(x_ref, g_ref, wg_ref, wu_ref, wd_ref, o_ref, h_scr):
    @pl.when(pl.program_id(1) == 0)
    def _():
        x = x_ref[...]
        ms = jnp.mean(x * x, axis=-1, keepdims=True)
        h_scr[...] = (x * lax.rsqrt(ms + NORM_EPS) * g_ref[...]).astype(BF16)
        o_ref[...] = x

    h = h_scr[...]
    gate = _dot(h, wg_ref[...])
    up = _dot(h, wu_ref[...])
    act = (0.5 * _silu(gate) * up).astype(BF16)
    o_ref[...] += _dot(act, wd_ref[...])


def _ffn(x, g, l, wg, wu, wd, lw):
    m, d = x.shape
    f = wg.shape[-1]
    tm = _pick_tile(m, (512, 256, 128, 64, 32, 16))
    tf = FFN_TF
    return pl.pallas_call(
        _ffn_body,
        out_shape=jax.ShapeDtypeStruct((m, d), F32),
        grid=(m // tm, f // tf),
        in_specs=[pl.BlockSpec((tm, d), lambda i, j: (i, 0), pipeline_mode=pl.Buffered(1)),
                  pl.BlockSpec((None, 1, d), lambda i, j: (l, 0, 0)),
                  pl.BlockSpec((None, d, tf), lambda i, j: (lw, 0, j)),
                  pl.BlockSpec((None, d, tf), lambda i, j: (lw, 0, j)),
                  pl.BlockSpec((None, tf, d), lambda i, j: (lw, j, 0))],
        out_specs=pl.BlockSpec((tm, d), lambda i, j: (i, 0)),
        scratch_shapes=[pltpu.VMEM((tm, d), BF16)],
        compiler_params=_cparams(("arbitrary", "arbitrary"), VMEM_BIG),
        name="ffn",
    )(x, g, wg, wu, wd)


def _ffn_cast_body(x_ref, g_ref, wg_ref, wu_ref, wd_ref, o_ref, wgb_ref, wub_ref, wdb_ref, h_scr):
    wg, wu, wd = wg_ref[...].astype(BF16), wu_ref[...].astype(BF16), wd_ref[...].astype(BF16)
    wgb_ref[...] = wg
    wub_ref[...] = wu
    wdb_ref[...] = wd

    @pl.when(pl.program_id(0) == 0)
    def _():
        x = x_ref[...]
        ms = jnp.mean(x * x, axis=-1, keepdims=True)
        h_scr[...] = (x * lax.rsqrt(ms + NORM_EPS) * g_ref[...]).astype(BF16)
        o_ref[...] = x

    h = h_scr[...]
    act = (0.5 * _silu(_dot(h, wg)) * _dot(h, wu)).astype(BF16)
    o_ref[...] += _dot(act, wd)


def _ffn_cast(x, g, l, wg, wu, wd):
    m, d = x.shape
    f = wg.shape[-1]
    tf = FFN_TF
    up = pl.BlockSpec((None, d, tf), lambda j: (l, 0, j))
    up_b = pl.BlockSpec((None, d, tf), lambda j: (0, 0, j))
    row = pl.BlockSpec((m, d), lambda j: (0, 0))
    return pl.pallas_call(
        _ffn_cast_body,
        out_shape=(jax.ShapeDtypeStruct((m, d), F32), jax.ShapeDtypeStruct((1, d, f), BF16),
                   jax.ShapeDtypeStruct((1, d, f), BF16), jax.ShapeDtypeStruct((1, f, d), BF16)),
        grid=(f // tf,),
        in_specs=[row, pl.BlockSpec((None, 1, d), lambda j: (l, 0, 0)), up, up,
                  pl.BlockSpec((None, tf, d), lambda j: (l, j, 0))],
        out_specs=(row, up_b, up_b, pl.BlockSpec((None, tf, d), lambda j: (0, j, 0))),
        scratch_shapes=[pltpu.VMEM((m, d), BF16)],
        compiler_params=_cparams(("arbitrary",), VMEM_BIG),
        name="ffn_sample",
    )(x, g, wg, wu, wd)


def _mm_body(x_ref, w_ref, *rest, residual):
    acc = _dot(x_ref[...], w_ref[...])
    if residual:
        acc = rest[0][...] + acc
        rest = rest[1:]
    rest[0][...] = acc


def _mm(x, w, l, residual=None):
    m, k = x.shape
    n = w.shape[-1]
    tm = _pick_tile(m, (1024, 512, 256, 128, 64, 32, 16))
    tn = _pick_tile(n, (1024, 512, 256, 128))
    in_specs = [pl.BlockSpec((tm, k), lambda i, j: (i, 0)),
                pl.BlockSpec((None, k, tn), lambda i, j: (l, 0, j))]
    args = [x, w]
    if residual is not None:
        in_specs.append(pl.BlockSpec((tm, tn), lambda i, j: (i, j)))
        args.append(residual)
    return pl.pallas_call(
        functools.partial(_mm_body, residual=residual is not None),
        out_shape=jax.ShapeDtypeStruct((m, n), F32),
        grid=(m // tm, n // tn),
        in_specs=in_specs,
        out_specs=pl.BlockSpec((tm, tn), lambda i, j: (i, j)),
        compiler_params=_cparams(("arbitrary", "arbitrary"), VMEM_BIG),
        name="matmul",
    )(*args)


def _proj_body(x_ref, w_ref, o_ref, *rest):
    acc = _dot_nt(x_ref[...], w_ref[0].astype(BF16))
    o_ref[...] = acc
    if rest:
        rest[0][...] = acc.astype(BF16)


def _proj(x, w_t, l, start, n, bf16_copy=False):
    m, k = x.shape
    assert start % BF16_ROWS == 0 and start + n <= w_t.shape[1]
    tm = _pick_tile(m, (2048, 1024, 512, 256, 128, 64, 32, 16))
    tn = _pick_tile(n, (256, 128))
    out_shape = [jax.ShapeDtypeStruct((m, n), F32)]
    out_specs = [pl.BlockSpec((tm, tn), lambda i, j: (i, j))]
    if bf16_copy:
        out_shape.append(jax.ShapeDtypeStruct((m, n), BF16))
        out_specs.append(pl.BlockSpec((tm, tn), lambda i, j: (i, j)))
    out = pl.pallas_call(
        _proj_body,
        out_shape=out_shape,
        grid=(m // tm, n // tn),
        in_specs=[pl.BlockSpec((tm, k), lambda i, j: (i, 0)),
                  pl.BlockSpec((pl.Element(1), pl.Element(tn), pl.Element(k)),
                               lambda i, j: (l, pl.multiple_of(start + j * tn, BF16_ROWS), 0))],
        out_specs=out_specs,
        compiler_params=_cparams(("arbitrary", "arbitrary"), VMEM_PROJ),
        name="in_proj",
    )(x, w_t)
    return out if bf16_copy else out[0]


def _merge_body(oa_ref, ob_ref, oc_ref, w_ref, g0_ref, g1_ref, g2_ref, o_ref):
    acc = jax.nn.sigmoid(g0_ref[...]) * _dot(oa_ref[...], w_ref[0])
    acc += jax.nn.sigmoid(g1_ref[...]) * _dot(ob_ref[...], w_ref[1])
    acc += jax.nn.sigmoid(g2_ref[...]) * _dot(oc_ref[...], w_ref[2])
    o_ref[...] = acc.astype(BF16)


def _merge(o_a, o_b, o_c, w_br3, ug, l):
    m, c = o_a.shape
    d = w_br3.shape[-1]
    tm = _pick_tile(m, (1024, 512, 256, 128, 64, 32, 16))
    tn = _pick_tile(d, (256, 128))
    nj = d // tn
    o_spec = pl.BlockSpec((tm, c), lambda i, j: (i, 0))
    return pl.pallas_call(
        _merge_body,
        out_shape=jax.ShapeDtypeStruct((m, d), BF16),
        grid=(m // tm, nj),
        in_specs=[o_spec, o_spec, o_spec,
                  pl.BlockSpec((None, 3, c, tn), lambda i, j: (l, 0, 0, j)),
                  pl.BlockSpec((tm, tn), lambda i, j: (i, j)),
                  pl.BlockSpec((tm, tn), lambda i, j: (i, nj + j)),
                  pl.BlockSpec((tm, tn), lambda i, j: (i, 2 * nj + j))],
        out_specs=pl.BlockSpec((tm, tn), lambda i, j: (i, j)),
        compiler_params=_cparams(("arbitrary", "arbitrary"), VMEM_BIG),
        name="merge",
    )(o_a, o_b, o_c, w_br3, ug, ug, ug)


def _rwkv_scan_body(r_ref, k_ref, v_ref, tl_ref, mu_r, mu_k, mu_v, mu_t, wl_w, wl_a, wl_g,
                    sh_r, sh_k, sh_v, sh_t, par_ref, s0_ref, o_ref, so_ref, s_scr, c_r, c_k, c_v, c_t,
                    *, t, valid, nchunks, nseq, npair, lw, la):
    c = pl.program_id(2)

    @pl.when(c == 0)
    def _():
        s_scr[...] = s0_ref[...]
        for carry, first in ((c_r, sh_r), (c_k, sh_k), (c_v, sh_v), (c_t, sh_t)):
            carry[...] = first[...]

    hd = LANES // 2
    lane = lax.broadcasted_iota(jnp.int32, (t, LANES), 1)
    m0 = lane < hd
    ri = lax.broadcasted_iota(jnp.int32, (2 * t, 2 * t), 0)
    ci = lax.broadcasted_iota(jnp.int32, (2 * t, 2 * t), 1)
    same_head = (ri // t) == (ci // t)
    strict = jnp.logical_and(same_head, ci < ri)
    incl = jnp.logical_and(same_head, ci <= ri)
    eye = (ci == ri).astype(F32)
    bi = lax.broadcasted_iota(jnp.int32, (LANES, LANES), 0) // hd
    bj = lax.broadcasted_iota(jnp.int32, (LANES, LANES), 1) // hd
    head_diag = bi == bj
    bf = lambda x: x.astype(BF16)
    twice = lambda x: jnp.concatenate([x, x], axis=0)

    def seg_sum(x):
        s_lo = jnp.sum(jnp.where(m0, x, 0.0), axis=1, keepdims=True)
        s_hi = jnp.sum(jnp.where(m0, 0.0, x), axis=1, keepdims=True)
        return jnp.where(m0, s_lo, s_hi)

    def shift_lerp(x_ref, carry, mu_ref, i):
        x = x_ref[i]
        row = lax.broadcasted_iota(jnp.int32, x.shape, 0)
        prev = jnp.where(row == 0, carry[i], pltpu.roll(x, 1, axis=0))
        carry[i] = x[t - 1:t, :]
        return x + mu_ref[...] * (prev - x)

    seqs = []
    for i in range(nseq):
        r, k, v = shift_lerp(r_ref, c_r, mu_r, i), shift_lerp(k_ref, c_k, mu_k, i), shift_lerp(v_ref, c_v, mu_v, i)
        tail = shift_lerp(tl_ref, c_t, mu_t, i)
        tlane = lax.broadcasted_iota(jnp.int32, tail.shape, 1)
        act = bf(jnp.where(tlane < lw, jnp.tanh(tail), jnp.where(tlane < lw + la, tail, jax.nn.sigmoid(tail))))
        seqs.append((r, k, v, _dot(act, wl_w[...]), _dot(act, wl_a[...]), _dot(act, wl_g[...])))

    chains = [(i, p) for i in range(nseq) for p in range(npair)]
    blk = lambda which, i, p: seqs[i][which][:, p * LANES:(p + 1) * LANES]
    i_r, i_k, i_v, i_wl, i_al, i_g = range(6)

    def stage(f, *cols):
        return [f(*a) for a in zip(*cols)]

    def elementwise(i, p):
        par = par_ref[:, p * LANES:(p + 1) * LANES]
        w0, a0, kk_w, ka_w = (par[j:j + 1] for j in range(4))
        k = blk(i_k, i, p)
        v = blk(i_v, i, p)
        w = -_softplus(-(w0 + blk(i_wl, i, p))) - 0.5
        logd = -jnp.exp(w)
        a = jax.nn.sigmoid(a0 + blk(i_al, i, p))
        kkr = k * kk_w
        kk = kkr / jnp.maximum(jnp.sqrt(seg_sum(kkr * kkr)), 1e-12)
        k2 = k * (1.0 + (a - 1.0) * ka_w)
        if valid < t * nchunks:
            ok = c * t + lax.broadcasted_iota(jnp.int32, (t, LANES), 0) < valid
            logd = jnp.where(ok, logd, 0.0)
            kk = jnp.where(ok, kk, 0.0)
            k2m = jnp.where(ok, k2, 0.0)
            vm = jnp.where(ok, v, 0.0)
        else:
            k2m, vm = k2, v
        cs = _cumsum_rows(logd)
        gam = jnp.exp(cs)
        ginv = jnp.exp(-cs)
        rt = blk(i_r, i, p) * gam
        at = -kk * jnp.exp(cs - logd)
        bt = bf(kk * a * ginv)
        kt = bf(k2m * ginv)
        return rt, at, bt, kt, vm, k2, gam[t - 1:t, :]

    rts, ats, bts, kts, vms, k2s, g_lasts = zip(*[elementwise(i, p) for i, p in chains])

    grams = stage(lambda at, rt, bt, kt: _dot_nt(
        bf(jnp.concatenate([jnp.where(m0, at, 0.0), jnp.where(m0, 0.0, at),
                            jnp.where(m0, rt, 0.0), jnp.where(m0, 0.0, rt)], axis=0)),
        jnp.concatenate([bt, bt, kt, kt], axis=0)), ats, rts, bts, kts)
    a_aks = stage(lambda g: bf(jnp.where(strict, g[:2 * t, 2 * t:], 0.0)), grams)
    m_rbs = stage(lambda g: bf(jnp.where(incl, g[2 * t:, :2 * t], 0.0)), grams)
    m_rks = stage(lambda g: bf(jnp.where(incl, g[2 * t:, 2 * t:], 0.0)), grams)
    npows = stage(lambda g: jnp.where(strict, g[:2 * t, :2 * t], 0.0), grams)
    invs = stage(lambda x: eye + x, npows)
    for _ in range(int(math.log2(t)) - 1):
        npows = stage(lambda x: _dot(bf(x), bf(x)), npows)
        invs = stage(lambda iv, x: iv + _dot(bf(iv), bf(x)), invs, npows)

    s_prevs = [s_scr[i, p] for i, p in chains]
    s_bs = stage(bf, s_prevs)
    v2s = stage(lambda vm: bf(twice(vm)), vms)
    zss = stage(lambda at, sb: _dot_nt(bf(at), sb), ats, s_bs)
    rss = stage(lambda rt, sb: _dot_nt(bf(rt), sb), rts, s_bs)
    rhss = stage(lambda zs, ak, v2: bf(twice(zs) + _dot(ak, v2)), zss, a_aks, v2s)
    w_sts = stage(lambda iv, x: _dot(bf(iv), x), invs, rhss)
    y_sts = stage(lambda rs, rb, w, rk, v2: twice(rs) + _dot(rb, bf(w)) + _dot(rk, v2), rss, m_rbs, w_sts, m_rks, v2s)
    wmats = stage(lambda w: jnp.where(m0, w[:t], w[t:]), w_sts)
    upds = stage(lambda w, vm, bt, kt: _dot(bf(jnp.concatenate([w, vm], axis=0).T), jnp.concatenate([bt, kt], axis=0)),
                 wmats, vms, bts, kts)
    for (i, p), s_prev, upd, g_last in zip(chains, s_prevs, upds, g_lasts):
        s_scr[i, p] = (s_prev + jnp.where(head_diag, upd, 0.0)) * g_last

    inv_hd = 1.0 / hd
    for (i, p), y_st, k2 in zip(chains, y_sts, k2s):
        par = par_ref[:, p * LANES:(p + 1) * LANES]
        rk_w, ln_w, ln_b = (par[j:j + 1] for j in range(4, 7))
        y = jnp.where(m0, y_st[:t], y_st[t:])
        yc = y - seg_sum(y) * inv_hd
        yn = yc * lax.rsqrt(seg_sum(yc * yc) * inv_hd + LN_X_EPS) * ln_w + ln_b
        bonus = seg_sum(blk(i_r, i, p) * k2 * rk_w) * blk(i_v, i, p)
        o_ref[i, :, p * LANES:(p + 1) * LANES] = ((yn + bonus) * blk(i_g, i, p)).astype(BF16)

    @pl.when(c == nchunks - 1)
    def _():
        so_ref[...] = s_scr[...]


def _rwkv_scan(ua3, shift0, mu, wl, par, s0, l, valid, c3, lw, la):
    b, lseq, nap = ua3.shape
    c = c3 // 3
    tail_w = nap - c3
    assert c3 % tail_w == 0
    t = RWKV_CHUNK
    nchunks = lseq // t
    nseq = _pick_tile(b, (4, 2, 1))
    npair = _pick_tile(c // LANES, (RWKV_CHAINS // nseq, 1))
    w = npair * LANES
    ngrp = c // w
    tcol = c3 // tail_w

    def col(rows, off):
        return pl.BlockSpec((nseq, rows, w), lambda bi, p, ci: (bi, ci if rows == t else 0, off * ngrp + p))

    def tail(rows):
        return pl.BlockSpec((nseq, rows, tail_w), lambda bi, p, ci: (bi, ci if rows == t else 0, tcol))

    def per_layer(rows, off):
        return pl.BlockSpec((None, rows, w), lambda bi, p, ci: (l, 0, off * ngrp + p))

    state = pl.BlockSpec((nseq, npair, LANES, LANES), lambda bi, p, ci: (bi, p, 0, 0))
    o_a, s_out = pl.pallas_call(
        functools.partial(_rwkv_scan_body, t=t, valid=valid, nchunks=nchunks, nseq=nseq, npair=npair, lw=lw, la=la),
        out_shape=(jax.ShapeDtypeStruct((b, lseq, c), BF16),
                   jax.ShapeDtypeStruct((b, c // LANES, LANES, LANES), F32)),
        grid=(b // nseq, ngrp, nchunks),
        in_specs=[col(t, 0), col(t, 1), col(t, 2), tail(t),
                  per_layer(1, 0), per_layer(1, 1), per_layer(1, 2),
                  pl.BlockSpec((None, 1, tail_w), lambda bi, p, ci: (l, 0, tcol)),
                  per_layer(tail_w, 0), per_layer(tail_w, 1), per_layer(tail_w, 2),
                  col(1, 0), col(1, 1), col(1, 2), tail(1),
                  per_layer(8, 0), state],
        out_specs=(pl.BlockSpec((nseq, t, w), lambda bi, p, ci: (bi, ci, p)), state),
        scratch_shapes=[pltpu.VMEM((nseq, npair, LANES, LANES), F32),
                        pltpu.VMEM((nseq, 1, w), F32), pltpu.VMEM((nseq, 1, w), F32),
                        pltpu.VMEM((nseq, 1, w), F32), pltpu.VMEM((nseq, 1, tail_w), F32)],
        compiler_params=_cparams(("arbitrary", "arbitrary", "arbitrary"), VMEM_SMALL),
        name="rwkv_scan",
    )(ua3, ua3, ua3, ua3, mu, mu, mu, mu, wl, wl, wl, shift0, shift0, shift0, shift0, par, s0)
    return o_a.reshape(b * lseq, c), s_out


def _qk_prep_body(uq_ref, uk_ref, cos_ref, sin_ref, qn_ref, kn_ref, q_ref, k_ref, kb_ref, *, dh, scale):
    cos = cos_ref[...]
    sin = sin_ref[...]
    shape = cos.shape
    lane = lax.broadcasted_iota(jnp.int32, shape, 1)
    first_half = (lane % dh) < (dh // 2)
    si = lax.broadcasted_iota(jnp.int32, (LANES, LANES), 0) // dh
    sj = lax.broadcasted_iota(jnp.int32, (LANES, LANES), 1) // dh
    seg = (si == sj).astype(BF16)

    def prep(x, w):
        hi, lo = _split_bf16(x * x)
        ss = _dot(hi, seg) + _dot(lo, seg)
        y = x * lax.rsqrt(ss * (1.0 / dh) + NORM_EPS) * w
        partner = jnp.where(first_half, pltpu.roll(y, LANES - dh // 2, axis=1), pltpu.roll(y, dh // 2, axis=1))
        return y * cos + partner * sin

    q_ref[...] = prep(uq_ref[...], qn_ref[...]) * scale
    kr = prep(uk_ref[...], kn_ref[...])
    k_ref[...] = kr
    kb_ref[...] = kr.astype(BF16)


def _qk_prep(uq, uk, cos, sin, qn, kn, l, dh):
    m, cq = uq.shape
    tm = cos.shape[0]
    nh = cq // LANES
    nrep = m // tm
    blk = pl.BlockSpec((tm, LANES), lambda i, h: (i, h))
    tab = pl.BlockSpec((tm, LANES), lambda i, h: (0, 0))
    nrm = pl.BlockSpec((None, 1, LANES), lambda i, h: (l, 0, 0))
    return pl.pallas_call(
        functools.partial(_qk_prep_body, dh=dh, scale=dh ** -0.5),
        out_shape=(jax.ShapeDtypeStruct((m, cq), F32), jax.ShapeDtypeStruct((m, cq), F32),
                   jax.ShapeDtypeStruct((m, cq), BF16)),
        grid=(nrep, nh),
        in_specs=[blk, blk, tab, tab, nrm, nrm],
        out_specs=(blk, blk, blk),
        compiler_params=_cparams(("arbitrary", "arbitrary"), VMEM_SMALL),
        name="qk_prep",
    )(uq, uk, cos, sin, qn, kn)


def _diff_lambda(lp, lam_init):
    s1 = jnp.sum(lp[0:1] * lp[1:2], axis=1, keepdims=True)
    s2 = jnp.sum(lp[2:3] * lp[3:4], axis=1, keepdims=True)
    return jnp.exp(s1) - jnp.exp(s2) + lam_init


def _sub_ln(o, sub, lam_init):
    ms = jnp.mean(o * o, axis=-1, keepdims=True)
    return o * lax.rsqrt(ms + NORM_EPS) * sub * (1.0 - lam_init)


def _attn_prompt_body(q_ref, k_ref, v_ref, lam_ref, sub_ref, o_ref, *, lam_init, dh, tq):
    lseq = q_ref.shape[0]
    lane = lax.broadcasted_iota(jnp.int32, (tq, LANES), 1)
    tri = lax.broadcasted_iota(jnp.int32, (tq, tq), 1) <= lax.broadcasted_iota(jnp.int32, (tq, tq), 0)
    lam = _diff_lambda(lam_ref[...], lam_init)
    sub = sub_ref[...]
    for qi in range(lseq // tq):
        lo, hi = qi * tq, (qi + 1) * tq
        q = q_ref[lo:hi, :]
        k_diag = k_ref[lo:hi, :]
        v_diag = v_ref[lo:hi, :]
        outs = []
        for qc in (jnp.where(lane < dh, q, 0.0).astype(BF16), jnp.where(lane < dh, 0.0, q).astype(BF16)):
            s_diag = jnp.where(tri, _dot_nt(qc, k_diag), -jnp.inf)
            m = jnp.max(s_diag, axis=1, keepdims=True)
            if qi:
                s_off = _dot_nt(qc, k_ref[0:lo, :])
                m = jnp.maximum(m, jnp.max(s_off, axis=1, keepdims=True))
            p_diag = jnp.exp(s_diag - m)
            l = jnp.sum(p_diag, axis=1, keepdims=True)
            o = _dot(p_diag.astype(BF16), v_diag)
            if qi:
                p_off = jnp.exp(s_off - m)
                l = l + jnp.sum(p_off, axis=1, keepdims=True)
                o = o + _dot(p_off.astype(BF16), v_ref[0:lo, :])
            outs.append(o * (1.0 / l))
        o_ref[lo:hi, :] = _sub_ln(outs[0] - lam * outs[1], sub, lam_init).astype(BF16)


def _attn_prompt(q, kb, vb, lam_p, sub, l, b, lseq, lam_init, dh):
    m, cq = q.shape
    nh = cq // LANES
    tq = _pick_tile(lseq, (256, 128))
    blk = pl.BlockSpec((lseq, LANES), lambda bi, h: (bi, h))
    return pl.pallas_call(
        functools.partial(_attn_prompt_body, lam_init=lam_init, dh=dh, tq=tq),
        out_shape=jax.ShapeDtypeStruct((m, cq), BF16),
        grid=(b, nh),
        in_specs=[blk, blk, blk,
                  pl.BlockSpec((None,) + lam_p.shape[1:], lambda bi, h: (l, 0, 0)),
                  pl.BlockSpec((None, 1, LANES), lambda bi, h: (l, 0, 0))],
        out_specs=blk,
        compiler_params=_cparams(("arbitrary", "arbitrary"), VMEM_SMALL),
        name="attn_prompt",
    )(q, kb, vb, lam_p, sub)


def _attn_sample_body(pt_ref, q_ref, *rest, lam_init, dh, nsteps, npp):
    del pt_ref
    kc_refs, vc_refs = rest[:npp], rest[npp:2 * npp]
    kcur_ref, vcur_ref, lam_ref, sub_ref, o_ref, m_scr, l_scr, acc_scr = rest[2 * npp:]
    p = pl.program_id(1)
    q = q_ref[...]
    nh = q.shape[0]
    lane = lax.broadcasted_iota(jnp.int32, q.shape, 1)
    q2 = jnp.concatenate([jnp.where(lane < dh, q, 0.0), jnp.where(lane < dh, 0.0, q)], axis=0)
    twice = lambda x: jnp.concatenate([x, x], axis=0)

    @pl.when(p == 0)
    def _():
        s_cur = jnp.sum(q2 * twice(kcur_ref[...]), axis=1, keepdims=True)
        m_scr[...] = jnp.broadcast_to(s_cur, m_scr.shape)
        l_scr[...] = jnp.ones_like(l_scr)
        acc_scr[...] = twice(vcur_ref[...])

    ps = kc_refs[0].shape[0]
    rows = ps * nh
    rid = lax.broadcasted_iota(jnp.int32, (2 * nh, rows), 0)
    cid = lax.broadcasted_iota(jnp.int32, (2 * nh, rows), 1)
    own_head = (cid % nh) == (rid % nh)
    q2b = q2.astype(BF16)
    ss = [jnp.where(own_head, _dot_nt(q2b, kc[...].reshape(rows, LANES).astype(BF16)), -jnp.inf) for kc in kc_refs]
    m_old = m_scr[:, 0:1]
    m_new = m_old
    for s_j in ss:
        m_new = jnp.maximum(m_new, jnp.max(s_j, axis=1, keepdims=True))
    alpha = jnp.exp(m_old - m_new)
    ps_j = [jnp.exp(s_j - m_new) for s_j in ss]
    l_new = alpha * l_scr[:, 0:1]
    acc = alpha * acc_scr[...]
    for p_j, vc in zip(ps_j, vc_refs):
        l_new = l_new + jnp.sum(p_j, axis=1, keepdims=True)
        acc = acc + _dot(p_j.astype(BF16), vc[...].reshape(rows, LANES).astype(BF16))
    m_scr[...] = jnp.broadcast_to(m_new, m_scr.shape)
    l_scr[...] = jnp.broadcast_to(l_new, l_scr.shape)
    acc_scr[...] = acc

    @pl.when(p == nsteps - 1)
    def _():
        lam = _diff_lambda(lam_ref[...], lam_init)
        o = acc[:nh] * (1.0 / l_new[:nh]) - lam * (acc[nh:] * (1.0 / l_new[nh:]))
        o_ref[...] = _sub_ln(o, sub_ref[...], lam_init)


def _attn_sample(page_table, q3, cache_k, cache_v, kcur3, vcur3, lam_p, sub, l, lam_init, dh):
    bs, nh, _ = q3.shape
    npages = page_table.shape[1]
    ps = cache_k.shape[2]
    npp = _pick_tile(npages, (SAMPLE_PAGES_PER_STEP, 2, 1))
    nsteps = npages // npp
    cur = pl.BlockSpec((None, nh, LANES), lambda bi, p, pt: (bi, 0, 0))

    def page(j):
        return pl.BlockSpec((None, None, ps, nh, LANES), lambda bi, p, pt: (l, pt[bi, p * npp + j], 0, 0, 0))

    pages = [page(j) for j in range(npp)]
    grid_spec = pltpu.PrefetchScalarGridSpec(
        num_scalar_prefetch=1,
        grid=(bs, nsteps),
        in_specs=[cur] + pages + pages + [
            cur, cur,
            pl.BlockSpec((None,) + lam_p.shape[1:], lambda bi, p, pt: (l, 0, 0)),
            pl.BlockSpec((None, 1, LANES), lambda bi, p, pt: (l, 0, 0))],
        out_specs=cur,
        scratch_shapes=[pltpu.VMEM((2 * nh, LANES), F32)] * 3,
    )
    return pl.pallas_call(
        functools.partial(_attn_sample_body, lam_init=lam_init, dh=dh, nsteps=nsteps, npp=npp),
        out_shape=jax.ShapeDtypeStruct((bs, nh, LANES), F32),
        grid_spec=grid_spec,
        compiler_params=_cparams(("arbitrary", "arbitrary"), VMEM_SMALL),
        name="attn_sample",
    )(page_table, q3, *([cache_k] * npp), *([cache_v] * npp), kcur3, vcur3, lam_p, sub)


def _conv_body(x_ref, c0_ref, w_ref, b_ref, o_ref, carry, *, width):
    x = x_ref[...]
    tm = x.shape[0]

    @pl.when(pl.program_id(1) == 0)
    def _():
        carry[...] = c0_ref[...]

    hist = carry[...]
    carry[...] = x[tm - 8:tm, :]
    w = w_ref[...]
    row8 = lax.broadcasted_iota(jnp.int32, hist.shape, 0)
    acc = b_ref[...] + x * w[width - 1:width]
    for j in range(1, width):
        rolled = pltpu.roll(x, j, axis=0)
        head = jnp.where(row8 < j, pltpu.roll(hist, j, axis=0), rolled[0:8])
        xj = jnp.concatenate([head, rolled[8:]], axis=0)
        acc = acc + xj * w[width - 1 - j:width - j]
    o_ref[...] = _silu(acc)


def _conv(x3, conv0_8, w, bias, l):
    b, lseq, cd = x3.shape
    width = w.shape[1]
    tm = _pick_tile(lseq, (256, 128))
    nblk = lseq // tm
    return pl.pallas_call(
        functools.partial(_conv_body, width=width),
        out_shape=jax.ShapeDtypeStruct((b * lseq, cd), F32),
        grid=(b, nblk),
        in_specs=[pl.BlockSpec((None, tm, cd), lambda bi, i: (bi, i, 0)),
                  pl.BlockSpec((None, 8, cd), lambda bi, i: (bi, 0, 0)),
                  pl.BlockSpec((None, width, cd), lambda bi, i: (l, 0, 0)),
                  pl.BlockSpec((None, 1, cd), lambda bi, i: (l, 0, 0))],
        out_specs=pl.BlockSpec((tm, cd), lambda bi, i: (bi * nblk + i, 0)),
        scratch_shapes=[pltpu.VMEM((8, cd), F32)],
        compiler_params=_cparams(("arbitrary", "arbitrary"), VMEM_SMALL),
        name="mamba_conv",
    )(x3, conv0_8, w, bias)


def _ssd_body(xbc_ref, z_ref, dt_ref, par_ref, nrm_ref, h0_ref, o_ref, ho_ref, st_scr,
              *, t, valid, nchunks, n_groups, heads_per_group, p_dim, n_dim):
    c = pl.program_id(1)

    @pl.when(c == 0)
    def _():
        st_scr[...] = h0_ref[...]

    par = par_ref[...]
    dt = _softplus(dt_ref[...] + par[0:1])
    if valid < t * nchunks:
        ok = c * t + lax.broadcasted_iota(jnp.int32, dt.shape, 0) < valid
        dt = jnp.where(ok, dt, 0.0)
    a_neg = -jnp.exp(par[1:2])
    d_skip = par[2:3]
    ri = lax.broadcasted_iota(jnp.int32, (t, t), 0)
    ci = lax.broadcasted_iota(jnp.int32, (t, t), 1)
    tril = ci <= ri
    acs = _cumsum_rows(dt * a_neg)
    acs_t = acs.T
    tot = acs[t - 1:t, :]
    lane = lax.broadcasted_iota(jnp.int32, (t, LANES), 1)
    m0 = lane < p_dim
    m0r = lax.broadcasted_iota(jnp.int32, (1, LANES), 1) < p_dim
    prow = lax.broadcasted_iota(jnp.int32, (LANES, n_dim), 0) < p_dim
    n_heads = n_groups * heads_per_group
    xs_w = n_heads * p_dim
    pairs_per_group = heads_per_group // 2
    gw = heads_per_group * p_dim

    for g in range(n_groups):
        b_g = xbc_ref[:, xs_w + g * n_dim: xs_w + (g + 1) * n_dim].astype(BF16)
        c_off = xs_w + n_groups * n_dim
        c_g = xbc_ref[:, c_off + g * n_dim: c_off + (g + 1) * n_dim].astype(BF16)
        cb = _dot_nt(c_g, b_g)
        ys = []
        for pp in range(pairs_per_group):
            pr = g * pairs_per_group + pp
            h_lo, h_hi = 2 * pr, 2 * pr + 1
            xs = xbc_ref[:, pr * LANES:(pr + 1) * LANES]
            col_lo, col_hi = acs[:, h_lo:h_lo + 1], acs[:, h_hi:h_hi + 1]
            l_lo = jnp.exp(jnp.where(tril, col_lo - acs_t[h_lo:h_lo + 1, :], -1e30))
            l_hi = jnp.exp(jnp.where(tril, col_hi - acs_t[h_hi:h_hi + 1, :], -1e30))
            x_dt = xs * jnp.where(m0, dt[:, h_lo:h_lo + 1], dt[:, h_hi:h_hi + 1])
            x_b = x_dt.astype(BF16)
            y_diag = jnp.where(m0, _dot((cb * l_lo).astype(BF16), x_b), _dot((cb * l_hi).astype(BF16), x_b))
            st = st_scr[pr]
            colp = jnp.where(m0, col_lo, col_hi)
            y_off = jnp.exp(colp) * _dot_nt(c_g, st.astype(BF16))
            totp = jnp.where(m0r, tot[:, h_lo:h_lo + 1], tot[:, h_hi:h_hi + 1])
            contrib = _dot((x_dt * jnp.exp(totp - colp)).T.astype(BF16), b_g)
            chunk_decay = jnp.where(prow, jnp.exp(tot[:, h_lo:h_lo + 1]), jnp.exp(tot[:, h_hi:h_hi + 1]))
            st_scr[pr] = chunk_decay * st + contrib
            dp = jnp.where(m0r, d_skip[:, h_lo:h_lo + 1], d_skip[:, h_hi:h_hi + 1])
            ys.append(y_diag + y_off + dp * xs)
        yg = ys[0] if len(ys) == 1 else jnp.concatenate(ys, axis=1)
        yz = yg * _silu(z_ref[:, g * gw:(g + 1) * gw])
        ms = jnp.mean(yz * yz, axis=-1, keepdims=True)
        o_ref[:, g * gw:(g + 1) * gw] = (yz * lax.rsqrt(ms + MB_NORM_EPS) * nrm_ref[:, g * gw:(g + 1) * gw]).astype(BF16)

    @pl.when(c == nchunks - 1)
    def _():
        ho_ref[...] = st_scr[...]


def _ssd(xbc, uz, udt, par, nrm, h0, l, b, lseq, valid, n_groups, heads_per_group, p_dim, n_dim):
    cd = xbc.shape[1]
    cc = uz.shape[1]
    t = SSD_CHUNK_T
    nchunks = lseq // t
    npair = h0.shape[1]
    return pl.pallas_call(
        functools.partial(_ssd_body, t=t, valid=valid, nchunks=nchunks, n_groups=n_groups,
                          heads_per_group=heads_per_group, p_dim=p_dim, n_dim=n_dim),
        out_shape=(jax.ShapeDtypeStruct((b * lseq, cc), BF16),
                   jax.ShapeDtypeStruct(h0.shape, F32)),
        grid=(b, nchunks),
        in_specs=[pl.BlockSpec((t, cd), lambda bi, ci: (bi * nchunks + ci, 0)),
                  pl.BlockSpec((t, cc), lambda bi, ci: (bi * nchunks + ci, 0)),
                  pl.BlockSpec((t, LANES), lambda bi, ci: (bi * nchunks + ci, 0)),
                  pl.BlockSpec((None, 8, LANES), lambda bi, ci: (l, 0, 0)),
                  pl.BlockSpec((None, 1, cc), lambda bi, ci: (l, 0, 0)),
                  pl.BlockSpec((None, npair, LANES, n_dim), lambda bi, ci: (bi, 0, 0, 0))],
        out_specs=(pl.BlockSpec((t, cc), lambda bi, ci: (bi * nchunks + ci, 0)),
                   pl.BlockSpec((None, npair, LANES, n_dim), lambda bi, ci: (bi, 0, 0, 0))),
        scratch_shapes=[pltpu.VMEM((npair, LANES, n_dim), F32)],
        compiler_params=_cparams(("arbitrary", "arbitrary"), VMEM_SMALL),
        name="ssd",
    )(xbc, uz, udt, par, nrm, h0)


def _pad_last(x, n):
    return jnp.pad(x, [(0, 0)] * (x.ndim - 1) + [(0, n - x.shape[-1])])


def _pad_rows(x, n):
    return jnp.pad(x, [(0, n - x.shape[0])] + [(0, 0)] * (x.ndim - 1))


def _wkv_to_blockdiag(wkv):
    b, h, hv, hk = wkv.shape
    w = wkv.reshape(b, h // 2, 2, hv, hk)
    z = jnp.zeros_like(w[:, :, 0])
    top = jnp.concatenate([w[:, :, 0], z], axis=-1)
    bot = jnp.concatenate([z, w[:, :, 1]], axis=-1)
    return jnp.concatenate([top, bot], axis=-2)


def _blockdiag_to_wkv(s, hv, hk):
    b, p = s.shape[:2]
    return jnp.stack([s[:, :, :hv, :hk], s[:, :, hv:, hk:]], axis=2).reshape(b, 2 * p, hv, hk)


def _first_token_rows(x_rows, bs, t):
    return jnp.pad(x_rows[:bs, None, :], ((0, 0), (0, t - 1), (0, 0)))


def kernel(x_prompt, x_sample, cache_k, cache_v, page_table, state_rwkv_shift, state_rwkv_wkv, state_conv, state_ssm, g_ffn1, w_ffn1_gate, w_ffn1_up, w_ffn1_down, g_mix, w_in, rw_mu, rw_w0, rw_w_up, rw_a0, rw_a_up, rw_g_up, rw_k_k, rw_k_a, rw_r_k, rw_ln_w, rw_ln_b, da_q_norm, da_k_norm, da_lambda, da_subln, mb_conv_w, mb_conv_b, mb_dt_bias, mb_A_log, mb_D, mb_norm, w_br, w_o, g_ffn2, w_ffn2_gate, w_ffn2_up, w_ffn2_down):
    bp, lp, d = x_prompt.shape
    bs, ls, _ = x_sample.shape
    depth = g_ffn1.shape[0]
    na = state_rwkv_shift.shape[-1]
    h_a, hd_a = state_rwkv_wkv.shape[2], state_rwkv_wkv.shape[3]
    c_a = h_a * hd_a
    lw, la, lg = rw_w_up.shape[1], rw_a_up.shape[1], rw_g_up.shape[1]
    h_b, cqk_h = cache_k.shape[3], cache_k.shape[4]
    vd_b = cache_v.shape[4]
    dh = cqk_h // 2
    c_qk, c_b = h_b * cqk_h, h_b * vd_b
    h_c, p_c, n_c = state_ssm.shape[2], state_ssm.shape[3], state_ssm.shape[4]
    c_c = h_c * p_c
    conv_w, conv_dim = mb_conv_w.shape[1], mb_conv_w.shape[2]
    g_c = (conv_dim - c_c) // (2 * n_c)
    e_c = h_c // g_c
    past = page_table.shape[1] * cache_k.shape[2]
    assert ls == 1, "sample group: one new token per sequence"
    assert hd_a * 2 == LANES and cqk_h == LANES and vd_b == LANES and p_c * 2 == LANES and n_c == LANES
    assert e_c % 2 == 0 and h_c <= LANES and conv_w <= 8
    assert na == 3 * c_a + lw + la + lg

    bf = lambda w: w.astype(BF16)
    nap = _round_up(na, LANES)
    seg_sizes = [na, c_qk, c_qk, c_b, c_c, conv_dim, h_c, 3 * d]
    offs = [0]
    for s in seg_sizes:
        offs.append(offs[-1] + s)
    assert offs[-1] == w_in.shape[-1]
    w_t = jnp.swapaxes(w_in, 1, 2)
    win = [(offs[i], _round_up(seg_sizes[i], LANES)) for i in range(len(seg_sizes))]
    assert win[-1][0] + win[-1][1] <= w_t.shape[1]
    w_br3 = bf(w_br).reshape(depth, 3, c_a, d)
    w_o_b = bf(w_o)
    tail_w = nap - 3 * c_a
    w_lora = jnp.zeros((depth, tail_w, 3 * c_a), BF16)
    w_lora = w_lora.at[:, :lw, :c_a].set(bf(rw_w_up))
    w_lora = w_lora.at[:, lw:lw + la, c_a:2 * c_a].set(bf(rw_a_up))
    w_lora = w_lora.at[:, lw + la:lw + la + lg, 2 * c_a:].set(bf(rw_g_up))

    row3 = lambda x: x.reshape(depth, 1, -1)
    g1, gm, g2 = row3(g_ffn1), row3(g_mix), row3(g_ffn2)
    mu = row3(_pad_last(rw_mu, nap))
    rw_par = jnp.stack([rw_w0, rw_a0, rw_k_k, rw_k_a, rw_r_k.reshape(depth, c_a), rw_ln_w, rw_ln_b,
                        jnp.zeros_like(rw_w0)], axis=1)
    qn = row3(da_q_norm)
    kn = row3(da_k_norm)
    sub = row3(da_subln)
    mb_par = jnp.stack([_pad_last(mb_dt_bias, LANES), _pad_last(mb_A_log, LANES), _pad_last(mb_D, LANES)]
                       + [jnp.zeros((depth, LANES), F32)] * 5, axis=1)
    mb_nrm = row3(mb_norm)
    conv_b = row3(mb_conv_b)

    half = dh // 2
    inv = ROPE_THETA ** (-jnp.arange(half, dtype=F32) / half)

    def rope_tables(pos):
        ang = pos.astype(F32)[:, None] * inv[None, :]
        cos, sin = jnp.cos(ang), jnp.sin(ang)
        return jnp.tile(cos, (1, LANES // half)), jnp.tile(jnp.concatenate([-sin, sin], axis=1), (1, LANES // dh))

    cos_p, sin_p = rope_tables(jnp.arange(lp))
    ms_rows = _round_up(bs * ls, BF16_ROWS)
    cos_s, sin_s = rope_tables(jnp.full((ms_rows,), past, jnp.int32))

    xp = x_prompt.reshape(bp * lp, d)
    xs = _pad_rows(x_sample.reshape(bs * ls, d), ms_rows)
    t_a, t_c = RWKV_CHUNK, SSD_CHUNK_T
    outs_p = [[] for _ in range(6)]
    outs_s = [[] for _ in range(6)]

    def project(x, l):
        h = _rmsnorm(x, gm, l)
        ua = _proj(h, w_t, l, *win[0])
        uq = _proj(h, w_t, l, *win[1])
        uk = _proj(h, w_t, l, *win[2])
        uv, uv_b = _proj(h, w_t, l, *win[3], bf16_copy=True)
        uz = _proj(h, w_t, l, *win[4])
        uxbc = _proj(h, w_t, l, *win[5])
        udt = _proj(h, w_t, l, *win[6])
        ug = _proj(h, w_t, l, *win[7])
        return ua, uq, uk, uv, uv_b, uz, uxbc, udt, ug

    def mix_out(x, o_a, o_b, o_c, ug, l):
        merged = _merge(o_a, o_b, o_c, w_br3, ug, l)
        return _mm(merged, w_o_b, l, residual=x)

    for l in range(depth):
        lam_init = 0.8 - 0.6 * math.exp(-0.3 * l)

        xs, wg_b, wu_b, wd_b = _ffn_cast(xs, g1, l, w_ffn1_gate, w_ffn1_up, w_ffn1_down)

        xp = _ffn(xp, g1, l, wg_b, wu_b, wd_b, 0)
        ua, uq, uk, uv, uv_b, uz, uxbc, udt, ug = project(xp, l)
        ua3 = ua.reshape(bp, lp, nap)
        o_a, s_out = _rwkv_scan(ua3, jnp.zeros((bp, 1, nap), F32), mu, w_lora, rw_par,
                                jnp.zeros((bp, c_a // LANES, LANES, LANES), F32), l, lp, 3 * c_a, lw, la)
        q_r, k_r, k_rb = _qk_prep(uq, uk, cos_p, sin_p, qn, kn, l, dh)
        o_b = _attn_prompt(q_r, k_rb, uv_b, da_lambda, sub, l, bp, lp, lam_init, dh)
        xbc_act = _conv(uxbc.reshape(bp, lp, conv_dim), jnp.zeros((bp, 8, conv_dim), F32), mb_conv_w, conv_b, l)
        o_c, h_out = _ssd(xbc_act, uz, udt, mb_par, mb_nrm, jnp.zeros((bp, h_c // 2, LANES, n_c), F32),
                          l, bp, lp, lp, g_c, e_c, p_c, n_c)
        xp = mix_out(xp, o_a, o_b, o_c, ug, l)
        outs_p[0].append(k_r.reshape(bp, lp, h_b, cqk_h))
        outs_p[1].append(uv.reshape(bp, lp, h_b, vd_b))
        outs_p[2].append(ua3[:, lp - 1, :na])
        outs_p[3].append(_blockdiag_to_wkv(s_out, hd_a, hd_a))
        outs_p[4].append(uxbc.reshape(bp, lp, conv_dim)[:, lp - (conv_w - 1):])
        outs_p[5].append(h_out.reshape(bp, h_c, p_c, n_c))

        ua, uq, uk, uv, uv_b, uz, uxbc, udt, ug = project(xs, l)
        ua3 = _first_token_rows(ua, bs, t_a)
        shift0 = _pad_last(state_rwkv_shift[l], nap)[:, None, :]
        o_a_seq, s_out = _rwkv_scan(ua3, shift0, mu, w_lora, rw_par, _wkv_to_blockdiag(state_rwkv_wkv[l]),
                                    l, ls, 3 * c_a, lw, la)
        o_a = _pad_rows(o_a_seq.reshape(bs, t_a, c_a)[:, 0], ms_rows)
        q_r, k_r, _ = _qk_prep(uq, uk, cos_s, sin_s, qn, kn, l, dh)
        o_b3 = _attn_sample(page_table, q_r[:bs].reshape(bs, h_b, cqk_h), cache_k, cache_v,
                            k_r[:bs].reshape(bs, h_b, cqk_h), uv[:bs].reshape(bs, h_b, vd_b),
                            da_lambda, sub, l, lam_init, dh)
        o_b = _pad_rows(o_b3.reshape(bs, c_b), ms_rows).astype(BF16)
        conv0 = state_conv[l]
        conv0_8 = jnp.pad(conv0, ((0, 0), (8 - (conv_w - 1), 0), (0, 0)))
        xbc_act = _conv(_first_token_rows(uxbc, bs, t_c), conv0_8, mb_conv_w, conv_b, l)
        h0 = state_ssm[l].reshape(bs, h_c // 2, LANES, n_c)
        o_c_seq, h_out = _ssd(xbc_act, _first_token_rows(uz, bs, t_c).reshape(bs * t_c, c_c),
                              _first_token_rows(udt, bs, t_c).reshape(bs * t_c, LANES),
                              mb_par, mb_nrm, h0, l, bs, t_c, ls, g_c, e_c, p_c, n_c)
        o_c = _pad_rows(o_c_seq.reshape(bs, t_c, c_c)[:, 0], ms_rows)
        xs = mix_out(xs, o_a, o_b, o_c, ug, l)
        xs, wg_b, wu_b, wd_b = _ffn_cast(xs, g2, l, w_ffn2_gate, w_ffn2_up, w_ffn2_down)
        xp = _ffn(xp, g2, l, wg_b, wu_b, wd_b, 0)
        outs_s[0].append(k_r[:bs].reshape(bs, ls, h_b, cqk_h))
        outs_s[1].append(uv[:bs].reshape(bs, ls, h_b, vd_b))
        outs_s[2].append(ua[:bs, :na])
        outs_s[3].append(_blockdiag_to_wkv(s_out, hd_a, hd_a))
        outs_s[4].append(jnp.concatenate([conv0, uxbc[:bs, None, :]], axis=1)[:, ls:])
        outs_s[5].append(h_out.reshape(bs, h_c, p_c, n_c))

    yp = xp.reshape(bp, lp, d)
    ys = xs[:bs * ls].reshape(bs, ls, d)
    return (yp, ys) + tuple(jnp.stack(o) for o in outs_p) + tuple(jnp.stack(o) for o in outs_s)
```

```python
import functools
import math

import jax
import jax.numpy as jnp
from jax import lax
from jax.experimental import pallas as pl
from jax.experimental.pallas import tpu as pltpu

F32 = jnp.float32
BF16 = jnp.bfloat16

NORM_EPS = 1e-6
ROPE_THETA = 10000.0
LN_X_EPS = 64e-5
MB_NORM_EPS = 1e-5

LANES = 128
BF16_ROWS = 16
MIB = 1024 * 1024
VMEM_BIG = 56 * MIB
VMEM_SMALL = 40 * MIB
VMEM_PROJ = 60 * MIB

RWKV_CHAINS = 16
SAMPLE_PAGES_PER_STEP = 8
RWKV_CHUNK = 64
SSD_CHUNK_T = 128
FFN_TF = 256


def _cparams(sem, vmem):
    return pltpu.CompilerParams(dimension_semantics=sem, vmem_limit_bytes=vmem)


def _round_up(n, m):
    return (n + m - 1) // m * m


def _pick_tile(n, cands):
    for c in cands:
        if n % c == 0:
            return c
    raise ValueError(f"no tile for {n} in {cands}")


def _dot(a, b):
    return jnp.dot(a, b, preferred_element_type=F32)


def _dot_nt(a, b):
    return lax.dot_general(a, b, (((1,), (1,)), ((), ())), preferred_element_type=F32)


def _softplus(x):
    return jnp.maximum(x, 0.0) + jnp.log1p(jnp.exp(-jnp.abs(x)))


def _silu(x):
    return x * jax.nn.sigmoid(x)


def _cumsum_rows(x):
    row = lax.broadcasted_iota(jnp.int32, x.shape, 0)
    shift = 1
    while shift < x.shape[0]:
        x = x + jnp.where(row >= shift, pltpu.roll(x, shift, axis=0), 0.0)
        shift *= 2
    return x


def _split_bf16(x):
    hi = x.astype(BF16)
    lo = (x - hi.astype(F32)).astype(BF16)
    return hi, lo


def _rmsnorm_body(x_ref, g_ref, o_ref):
    x = x_ref[...]
    ms = jnp.mean(x * x, axis=-1, keepdims=True)
    o_ref[...] = (x * lax.rsqrt(ms + NORM_EPS) * g_ref[...]).astype(BF16)


def _rmsnorm(x, g, l):
    m, d = x.shape
    tm = _pick_tile(m, (256, 128, 64, 32, 16))
    return pl.pallas_call(
        _rmsnorm_body,
        out_shape=jax.ShapeDtypeStruct((m, d), BF16),
        grid=(m // tm,),
        in_specs=[pl.BlockSpec((tm, d), lambda i: (i, 0)),
                  pl.BlockSpec((None, 1, d), lambda i: (l, 0, 0))],
        out_specs=pl.BlockSpec((tm, d), lambda i: (i, 0)),
        compiler_params=_cparams(("arbitrary",), VMEM_SMALL),
        name="rmsnorm",
    )(x, g)


def _ffn_body(x_ref, g_ref, wg_ref, wu_ref, wd_ref, o_ref, h_scr):
    @pl.when(pl.program_id(1) == 0)
    def _():
        x = x_ref[...]
        ms = jnp.mean(x * x, axis=-1, keepdims=True)
        h_scr[...] = (x * lax.rsqrt(ms + NORM_EPS) * g_ref[...]).astype(BF16)
        o_ref[...] = x

    h = h_scr[...]
    gate = _dot(h, wg_ref[...])
    up = _dot(h, wu_ref[...])
    act = (0.5 * _silu(gate) * up).astype(BF16)
    o_ref[...] += _dot(act, wd_ref[...])


def _ffn(x, g, l, wg, wu, wd, lw):
    m, d = x.shape
    f = wg.shape[-1]
    tm = _pick_tile(m, (512, 256, 128, 64, 32, 16))
    tf = FFN_TF
    return pl.pallas_call(
        _ffn_body,
        out_shape=jax.ShapeDtypeStruct((m, d), F32),
        grid=(m // tm, f // tf),
        in_specs=[pl.BlockSpec((tm, d), lambda i, j: (i, 0), pipeline_mode=pl.Buffered(1)),
                  pl.BlockSpec((None, 1, d), lambda i, j: (l, 0, 0)),
                  pl.BlockSpec((None, d, tf), lambda i, j: (lw, 0, j)),
                  pl.BlockSpec((None, d, tf), lambda i, j: (lw, 0, j)),
                  pl.BlockSpec((None, tf, d), lambda i, j: (lw, j, 0))],
        out_specs=pl.BlockSpec((tm, d), lambda i, j: (i, 0)),
        scratch_shapes=[pltpu.VMEM((tm, d), BF16)],
        compiler_params=_cparams(("arbitrary", "arbitrary"), VMEM_BIG),
        name="ffn",
    )(x, g, wg, wu, wd)


def _ffn_cast_body(x_ref, g_ref, wg_ref, wu_ref, wd_ref, o_ref, wgb_ref, wub_ref, wdb_ref, h_scr):
    wg, wu, wd = wg_ref[...].astype(BF16), wu_ref[...].astype(BF16), wd_ref[...].astype(BF16)
    wgb_ref[...] = wg
    wub_ref[...] = wu
    wdb_ref[...] = wd

    @pl.when(pl.program_id(0) == 0)
    def _():
        x = x_ref[...]
        ms = jnp.mean(x * x, axis=-1, keepdims=True)
        h_scr[...] = (x * lax.rsqrt(ms + NORM_EPS) * g_ref[...]).astype(BF16)
        o_ref[...] = x

    h = h_scr[...]
    act = (0.5 * _silu(_dot(h, wg)) * _dot(h, wu)).astype(BF16)
    o_ref[...] += _dot(act, wd)


def _ffn_cast(x, g, l, wg, wu, wd):
    m, d = x.shape
    f = wg.shape[-1]
    tf = FFN_TF
    up = pl.BlockSpec((None, d, tf), lambda j: (l, 0, j))
    up_b = pl.BlockSpec((None, d, tf), lambda j: (0, 0, j))
    row = pl.BlockSpec((m, d), lambda j: (0, 0))
    return pl.pallas_call(
        _ffn_cast_body,
        out_shape=(jax.ShapeDtypeStruct((m, d), F32), jax.ShapeDtypeStruct((1, d, f), BF16),
                   jax.ShapeDtypeStruct((1, d, f), BF16), jax.ShapeDtypeStruct((1, f, d), BF16)),
        grid=(f // tf,),
        in_specs=[row, pl.BlockSpec((None, 1, d), lambda j: (l, 0, 0)), up, up,
                  pl.BlockSpec((None, tf, d), lambda j: (l, j, 0))],
        out_specs=(row, up_b, up_b, pl.BlockSpec((None, tf, d), lambda j: (0, j, 0))),
        scratch_shapes=[pltpu.VMEM((m, d), BF16)],
        compiler_params=_cparams(("arbitrary",), VMEM_BIG),
        name="ffn_sample",
    )(x, g, wg, wu, wd)


def _mm_body(x_ref, w_ref, *rest, residual):
    acc = _dot(x_ref[...], w_ref[...])
    if residual:
        acc = rest[0][...] + acc
        rest = rest[1:]
    rest[0][...] = acc


def _mm(x, w, l, residual=None):
    m, k = x.shape
    n = w.shape[-1]
    tm = _pick_tile(m, (1024, 512, 256, 128, 64, 32, 16))
    tn = _pick_tile(n, (1024, 512, 256, 128))
    in_specs = [pl.BlockSpec((tm, k), lambda i, j: (i, 0)),
                pl.BlockSpec((None, k, tn), lambda i, j: (l, 0, j))]
    args = [x, w]
    if residual is not None:
        in_specs.append(pl.BlockSpec((tm, tn), lambda i, j: (i, j)))
        args.append(residual)
    return pl.pallas_call(
        functools.partial(_mm_body, residual=residual is not None),
        out_shape=jax.ShapeDtypeStruct((m, n), F32),
        grid=(m // tm, n // tn),
        in_specs=in_specs,
        out_specs=pl.BlockSpec((tm, tn), lambda i, j: (i, j)),
        compiler_params=_cparams(("arbitrary", "arbitrary"), VMEM_BIG),
        name="matmul",
    )(*args)


def _proj_body(x_ref, w_ref, o_ref, *rest):
    acc = _dot_nt(x_ref[...], w_ref[0].astype(BF16))
    o_ref[...] = acc
    if rest:
        rest[0][...] = acc.astype(BF16)


def _proj(x, w_t, l, start, n, bf16_copy=False):
    m, k = x.shape
    assert start % BF16_ROWS == 0 and start + n <= w_t.shape[1]
    tm = _pick_tile(m, (2048, 1024, 512, 256, 128, 64, 32, 16))
    tn = _pick_tile(n, (256, 128))
    out_shape = [jax.ShapeDtypeStruct((m, n), F32)]
    out_specs = [pl.BlockSpec((tm, tn), lambda i, j: (i, j))]
    if bf16_copy:
        out_shape.append(jax.ShapeDtypeStruct((m, n), BF16))
        out_specs.append(pl.BlockSpec((tm, tn), lambda i, j: (i, j)))
    out = pl.pallas_call(
        _proj_body,
        out_shape=out_shape,
        grid=(m // tm, n // tn),
        in_specs=[pl.BlockSpec((tm, k), lambda i, j: (i, 0)),
                  pl.BlockSpec((pl.Element(1), pl.Element(tn), pl.Element(k)),
                               lambda i, j: (l, pl.multiple_of(start + j * tn, BF16_ROWS), 0))],
        out_specs=out_specs,
        compiler_params=_cparams(("arbitrary", "arbitrary"), VMEM_PROJ),
        name="in_proj",
    )(x, w_t)
    return out if bf16_copy else out[0]


def _merge_body(oa_ref, ob_ref, oc_ref, w_ref, g0_ref, g1_ref, g2_ref, o_ref):
    acc = jax.nn.sigmoid(g0_ref[...]) * _dot(oa_ref[...], w_ref[0])
    acc += jax.nn.sigmoid(g1_ref[...]) * _dot(ob_ref[...], w_ref[1])
    acc += jax.nn.sigmoid(g2_ref[...]) * _dot(oc_ref[...], w_ref[2])
    o_ref[...] = acc.astype(BF16)


def _merge(o_a, o_b, o_c, w_br3, ug, l):
    m, c = o_a.shape
    d = w_br3.shape[-1]
    tm = _pick_tile(m, (1024, 512, 256, 128, 64, 32, 16))
    tn = _pick_tile(d, (256, 128))
    nj = d // tn
    o_spec = pl.BlockSpec((tm, c), lambda i, j: (i, 0))
    return pl.pallas_call(
        _merge_body,
        out_shape=jax.ShapeDtypeStruct((m, d), BF16),
        grid=(m // tm, nj),
        in_specs=[o_spec, o_spec, o_spec,
                  pl.BlockSpec((None, 3, c, tn), lambda i, j: (l, 0, 0, j)),
                  pl.BlockSpec((tm, tn), lambda i, j: (i, j)),
                  pl.BlockSpec((tm, tn), lambda i, j: (i, nj + j)),
                  pl.BlockSpec((tm, tn), lambda i, j: (i, 2 * nj + j))],
        out_specs=pl.BlockSpec((tm, tn), lambda i, j: (i, j)),
        compiler_params=_cparams(("arbitrary", "arbitrary"), VMEM_BIG),
        name="merge",
    )(o_a, o_b, o_c, w_br3, ug, ug, ug)


def _rwkv_scan_body(r_ref, k_ref, v_ref, tl_ref, mu_r, mu_k, mu_v, mu_t, wl_w, wl_a, wl_g,
                    sh_r, sh_k, sh_v, sh_t, par_ref, s0_ref, o_ref, so_ref, s_scr, c_r, c_k, c_v, c_t,
                    *, t, valid, nchunks, nseq, npair, lw, la):
    c = pl.program_id(2)

    @pl.when(c == 0)
    def _():
        s_scr[...] = s0_ref[...]
        for carry, first in ((c_r, sh_r), (c_k, sh_k), (c_v, sh_v), (c_t, sh_t)):
            carry[...] = first[...]

    hd = LANES // 2
    lane = lax.broadcasted_iota(jnp.int32, (t, LANES), 1)
    m0 = lane < hd
    ri = lax.broadcasted_iota(jnp.int32, (2 * t, 2 * t), 0)
    ci = lax.broadcasted_iota(jnp.int32, (2 * t, 2 * t), 1)
    same_head = (ri // t) == (ci // t)
    strict = jnp.logical_and(same_head, ci < ri)
    incl = jnp.logical_and(same_head, ci <= ri)
    eye = (ci == ri).astype(F32)
    bi = lax.broadcasted_iota(jnp.int32, (LANES, LANES), 0) // hd
    bj = lax.broadcasted_iota(jnp.int32, (LANES, LANES), 1) // hd
    head_diag = bi == bj
    bf = lambda x: x.astype(BF16)
    twice = lambda x: jnp.concatenate([x, x], axis=0)

    def seg_sum(x):
        s_lo = jnp.sum(jnp.where(m0, x, 0.0), axis=1, keepdims=True)
        s_hi = jnp.sum(jnp.where(m0, 0.0, x), axis=1, keepdims=True)
        return jnp.where(m0, s_lo, s_hi)

    def shift_lerp(x_ref, carry, mu_ref, i):
        x = x_ref[i]
        row = lax.broadcasted_iota(jnp.int32, x.shape, 0)
        prev = jnp.where(row == 0, carry[i], pltpu.roll(x, 1, axis=0))
        carry[i] = x[t - 1:t, :]
        return x + mu_ref[...] * (prev - x)

    seqs = []
    for i in range(nseq):
        r, k, v = shift_lerp(r_ref, c_r, mu_r, i), shift_lerp(k_ref, c_k, mu_k, i), shift_lerp(v_ref, c_v, mu_v, i)
        tail = shift_lerp(tl_ref, c_t, mu_t, i)
        tlane = lax.broadcasted_iota(jnp.int32, tail.shape, 1)
        act = bf(jnp.where(tlane < lw, jnp.tanh(tail), jnp.where(tlane < lw + la, tail, jax.nn.sigmoid(tail))))
        seqs.append((r, k, v, _dot(act, wl_w[...]), _dot(act, wl_a[...]), _dot(act, wl_g[...])))

    chains = [(i, p) for i in range(nseq) for p in range(npair)]
    blk = lambda which, i, p: seqs[i][which][:, p * LANES:(p + 1) * LANES]
    i_r, i_k, i_v, i_wl, i_al, i_g = range(6)

    def stage(f, *cols):
        return [f(*a) for a in zip(*cols)]

    def elementwise(i, p):
        par = par_ref[:, p * LANES:(p + 1) * LANES]
        w0, a0, kk_w, ka_w = (par[j:j + 1] for j in range(4))
        k = blk(i_k, i, p)
        v = blk(i_v, i, p)
        w = -_softplus(-(w0 + blk(i_wl, i, p))) - 0.5
        logd = -jnp.exp(w)
        a = jax.nn.sigmoid(a0 + blk(i_al, i, p))
        kkr = k * kk_w
        kk = kkr / jnp.maximum(jnp.sqrt(seg_sum(kkr * kkr)), 1e-12)
        k2 = k * (1.0 + (a - 1.0) * ka_w)
        if valid < t * nchunks:
            ok = c * t + lax.broadcasted_iota(jnp.int32, (t, LANES), 0) < valid
            logd = jnp.where(ok, logd, 0.0)
            kk = jnp.where(ok, kk, 0.0)
            k2m = jnp.where(ok, k2, 0.0)
            vm = jnp.where(ok, v, 0.0)
        else:
            k2m, vm = k2, v
        cs = _cumsum_rows(logd)
        gam = jnp.exp(cs)
        ginv = jnp.exp(-cs)
        rt = blk(i_r, i, p) * gam
        at = -kk * jnp.exp(cs - logd)
        bt = bf(kk * a * ginv)
        kt = bf(k2m * ginv)
        return rt, at, bt, kt, vm, k2, gam[t - 1:t, :]

    rts, ats, bts, kts, vms, k2s, g_lasts = zip(*[elementwise(i, p) for i, p in chains])

    grams = stage(lambda at, rt, bt, kt: _dot_nt(
        bf(jnp.concatenate([jnp.where(m0, at, 0.0), jnp.where(m0, 0.0, at),
                            jnp.where(m0, rt, 0.0), jnp.where(m0, 0.0, rt)], axis=0)),
        jnp.concatenate([bt, bt, kt, kt], axis=0)), ats, rts, bts, kts)
    a_aks = stage(lambda g: bf(jnp.where(strict, g[:2 * t, 2 * t:], 0.0)), grams)
    m_rbs = stage(lambda g: bf(jnp.where(incl, g[2 * t:, :2 * t], 0.0)), grams)
    m_rks = stage(lambda g: bf(jnp.where(incl, g[2 * t:, 2 * t:], 0.0)), grams)
    npows = stage(lambda g: jnp.where(strict, g[:2 * t, :2 * t], 0.0), grams)
    invs = stage(lambda x: eye + x, npows)
    for _ in range(int(math.log2(t)) - 1):
        npows = stage(lambda x: _dot(bf(x), bf(x)), npows)
        invs = stage(lambda iv, x: iv + _dot(bf(iv), bf(x)), invs, npows)

    s_prevs = [s_scr[i, p] for i, p in chains]
    s_bs = stage(bf, s_prevs)
    v2s = stage(lambda vm: bf(twice(vm)), vms)
    zss = stage(lambda at, sb: _dot_nt(bf(at), sb), ats, s_bs)
    rss = stage(lambda rt, sb: _dot_nt(bf(rt), sb), rts, s_bs)
    rhss = stage(lambda zs, ak, v2: bf(twice(zs) + _dot(ak, v2)), zss, a_aks, v2s)
    w_sts = stage(lambda iv, x: _dot(bf(iv), x), invs, rhss)
    y_sts = stage(lambda rs, rb, w, rk, v2: twice(rs) + _dot(rb, bf(w)) + _dot(rk, v2), rss, m_rbs, w_sts, m_rks, v2s)
    wmats = stage(lambda w: jnp.where(m0, w[:t], w[t:]), w_sts)
    upds = stage(lambda w, vm, bt, kt: _dot(bf(jnp.concatenate([w, vm], axis=0).T), jnp.concatenate([bt, kt], axis=0)),
                 wmats, vms, bts, kts)
    for (i, p), s_prev, upd, g_last in zip(chains, s_prevs, upds, g_lasts):
        s_scr[i, p] = (s_prev + jnp.where(head_diag, upd, 0.0)) * g_last

    inv_hd = 1.0 / hd
    for (i, p), y_st, k2 in zip(chains, y_sts, k2s):
        par = par_ref[:, p * LANES:(p + 1) * LANES]
        rk_w, ln_w, ln_b = (par[j:j + 1] for j in range(4, 7))
        y = jnp.where(m0, y_st[:t], y_st[t:])
        yc = y - seg_sum(y) * inv_hd
        yn = yc * lax.rsqrt(seg_sum(yc * yc) * inv_hd + LN_X_EPS) * ln_w + ln_b
        bonus = seg_sum(blk(i_r, i, p) * k2 * rk_w) * blk(i_v, i, p)
        o_ref[i, :, p * LANES:(p + 1) * LANES] = ((yn + bonus) * blk(i_g, i, p)).astype(BF16)

    @pl.when(c == nchunks - 1)
    def _():
        so_ref[...] = s_scr[...]


def _rwkv_scan(ua3, shift0, mu, wl, par, s0, l, valid, c3, lw, la):
    b, lseq, nap = ua3.shape
    c = c3 // 3
    tail_w = nap - c3
    assert c3 % tail_w == 0
    t = RWKV_CHUNK
    nchunks = lseq // t
    nseq = _pick_tile(b, (4, 2, 1))
    npair = _pick_tile(c // LANES, (RWKV_CHAINS // nseq, 1))
    w = npair * LANES
    ngrp = c // w
    tcol = c3 // tail_w

    def col(rows, off):
        return pl.BlockSpec((nseq, rows, w), lambda bi, p, ci: (bi, ci if rows == t else 0, off * ngrp + p))

    def tail(rows):
        return pl.BlockSpec((nseq, rows, tail_w), lambda bi, p, ci: (bi, ci if rows == t else 0, tcol))

    def per_layer(rows, off):
        return pl.BlockSpec((None, rows, w), lambda bi, p, ci: (l, 0, off * ngrp + p))

    state = pl.BlockSpec((nseq, npair, LANES, LANES), lambda bi, p, ci: (bi, p, 0, 0))
    o_a, s_out = pl.pallas_call(
        functools.partial(_rwkv_scan_body, t=t, valid=valid, nchunks=nchunks, nseq=nseq, npair=npair, lw=lw, la=la),
        out_shape=(jax.ShapeDtypeStruct((b, lseq, c), BF16),
                   jax.ShapeDtypeStruct((b, c // LANES, LANES, LANES), F32)),
        grid=(b // nseq, ngrp, nchunks),
        in_specs=[col(t, 0), col(t, 1), col(t, 2), tail(t),
                  per_layer(1, 0), per_layer(1, 1), per_layer(1, 2),
                  pl.BlockSpec((None, 1, tail_w), lambda bi, p, ci: (l, 0, tcol)),
                  per_layer(tail_w, 0), per_layer(tail_w, 1), per_layer(tail_w, 2),
                  col(1, 0), col(1, 1), col(1, 2), tail(1),
                  per_layer(8, 0), state],
        out_specs=(pl.BlockSpec((nseq, t, w), lambda bi, p, ci: (bi, ci, p)), state),
        scratch_shapes=[pltpu.VMEM((nseq, npair, LANES, LANES), F32),
                        pltpu.VMEM((nseq, 1, w), F32), pltpu.VMEM((nseq, 1, w), F32),
                        pltpu.VMEM((nseq, 1, w), F32), pltpu.VMEM((nseq, 1, tail_w), F32)],
        compiler_params=_cparams(("arbitrary", "arbitrary", "arbitrary"), VMEM_SMALL),
        name="rwkv_scan",
    )(ua3, ua3, ua3, ua3, mu, mu, mu, mu, wl, wl, wl, shift0, shift0, shift0, shift0, par, s0)
    return o_a.reshape(b * lseq, c), s_out


def _qk_prep_body(uq_ref, uk_ref, cos_ref, sin_ref, qn_ref, kn_ref, q_ref, k_ref, kb_ref, *, dh, scale):
    cos = cos_ref[...]
    sin = sin_ref[...]
    shape = cos.shape
    lane = lax.broadcasted_iota(jnp.int32, shape, 1)
    first_half = (lane % dh) < (dh // 2)
    si = lax.broadcasted_iota(jnp.int32, (LANES, LANES), 0) // dh
    sj = lax.broadcasted_iota(jnp.int32, (LANES, LANES), 1) // dh
    seg = (si == sj).astype(BF16)

    def prep(x, w):
        hi, lo = _split_bf16(x * x)
        ss = _dot(hi, seg) + _dot(lo, seg)
        y = x * lax.rsqrt(ss * (1.0 / dh) + NORM_EPS) * w
        partner = jnp.where(first_half, pltpu.roll(y, LANES - dh // 2, axis=1), pltpu.roll(y, dh // 2, axis=1))
        return y * cos + partner * sin

    q_ref[...] = prep(uq_ref[...], qn_ref[...]) * scale
    kr = prep(uk_ref[...], kn_ref[...])
    k_ref[...] = kr
    kb_ref[...] = kr.astype(BF16)


def _qk_prep(uq, uk, cos, sin, qn, kn, l, dh):
    m, cq = uq.shape
    tm = cos.shape[0]
    nh = cq // LANES
    nrep = m // tm
    blk = pl.BlockSpec((tm, LANES), lambda i, h: (i, h))
    tab = pl.BlockSpec((tm, LANES), lambda i, h: (0, 0))
    nrm = pl.BlockSpec((None, 1, LANES), lambda i, h: (l, 0, 0))
    return pl.pallas_call(
        functools.partial(_qk_prep_body, dh=dh, scale=dh ** -0.5),
        out_shape=(jax.ShapeDtypeStruct((m, cq), F32), jax.ShapeDtypeStruct((m, cq), F32),
                   jax.ShapeDtypeStruct((m, cq), BF16)),
        grid=(nrep, nh),
        in_specs=[blk, blk, tab, tab, nrm, nrm],
        out_specs=(blk, blk, blk),
        compiler_params=_cparams(("arbitrary", "arbitrary"), VMEM_SMALL),
        name="qk_prep",
    )(uq, uk, cos, sin, qn, kn)


def _diff_lambda(lp, lam_init):
    s1 = jnp.sum(lp[0:1] * lp[1:2], axis=1, keepdims=True)
    s2 = jnp.sum(lp[2:3] * lp[3:4], axis=1, keepdims=True)
    return jnp.exp(s1) - jnp.exp(s2) + lam_init


def _sub_ln(o, sub, lam_init):
    ms = jnp.mean(o * o, axis=-1, keepdims=True)
    return o * lax.rsqrt(ms + NORM_EPS) * sub * (1.0 - lam_init)


def _attn_prompt_body(q_ref, k_ref, v_ref, lam_ref, sub_ref, o_ref, *, lam_init, dh, tq):
    lseq = q_ref.shape[0]
    lane = lax.broadcasted_iota(jnp.int32, (tq, LANES), 1)
    tri = lax.broadcasted_iota(jnp.int32, (tq, tq), 1) <= lax.broadcasted_iota(jnp.int32, (tq, tq), 0)
    lam = _diff_lambda(lam_ref[...], lam_init)
    sub = sub_ref[...]
    for qi in range(lseq // tq):
        lo, hi = qi * tq, (qi + 1) * tq
        q = q_ref[lo:hi, :]
        k_diag = k_ref[lo:hi, :]
        v_diag = v_ref[lo:hi, :]
        qcs = (jnp.where(lane < dh, q, 0.0).astype(BF16), jnp.where(lane < dh, 0.0, q).astype(BF16))
        s_diags = [jnp.where(tri, _dot_nt(qc, k_diag), -jnp.inf) for qc in qcs]
        ms = [jnp.max(s, axis=1, keepdims=True) for s in s_diags]
        if qi:
            s_offs = [_dot_nt(qc, k_ref[0:lo, :]) for qc in qcs]
            ms = [jnp.maximum(m, jnp.max(s, axis=1, keepdims=True)) for m, s in zip(ms, s_offs)]
        p_diags = [jnp.exp(s - m) for s, m in zip(s_diags, ms)]
        ls = [jnp.sum(p, axis=1, keepdims=True) for p in p_diags]
        os_ = [_dot(p.astype(BF16), v_diag) for p in p_diags]
        if qi:
            p_offs = [jnp.exp(s - m) for s, m in zip(s_offs, ms)]
            ls = [l + jnp.sum(p, axis=1, keepdims=True) for l, p in zip(ls, p_offs)]
            os_ = [o + _dot(p.astype(BF16), v_ref[0:lo, :]) for o, p in zip(os_, p_offs)]
        outs = [o * (1.0 / l) for o, l in zip(os_, ls)]
        o_ref[lo:hi, :] = _sub_ln(outs[0] - lam * outs[1], sub, lam_init).astype(BF16)


def _attn_prompt(q, kb, vb, lam_p, sub, l, b, lseq, lam_init, dh):
    m, cq = q.shape
    nh = cq // LANES
    tq = _pick_tile(lseq, (512, 256, 128))
    blk = pl.BlockSpec((lseq, LANES), lambda bi, h: (bi, h))
    return pl.pallas_call(
        functools.partial(_attn_prompt_body, lam_init=lam_init, dh=dh, tq=tq),
        out_shape=jax.ShapeDtypeStruct((m, cq), BF16),
        grid=(b, nh),
        in_specs=[blk, blk, blk,
                  pl.BlockSpec((None,) + lam_p.shape[1:], lambda bi, h: (l, 0, 0)),
                  pl.BlockSpec((None, 1, LANES), lambda bi, h: (l, 0, 0))],
        out_specs=blk,
        compiler_params=_cparams(("arbitrary", "arbitrary"), VMEM_SMALL),
        name="attn_prompt",
    )(q, kb, vb, lam_p, sub)


def _attn_sample_body(pt_ref, q_ref, *rest, lam_init, dh, nsteps, npp):
    del pt_ref
    kc_refs, vc_refs = rest[:npp], rest[npp:2 * npp]
    kcur_ref, vcur_ref, lam_ref, sub_ref, o_ref, m_scr, l_scr, acc_scr = rest[2 * npp:]
    p = pl.program_id(1)
    q = q_ref[...]
    nh = q.shape[0]
    lane = lax.broadcasted_iota(jnp.int32, q.shape, 1)
    q2 = jnp.concatenate([jnp.where(lane < dh, q, 0.0), jnp.where(lane < dh, 0.0, q)], axis=0)
    twice = lambda x: jnp.concatenate([x, x], axis=0)

    @pl.when(p == 0)
    def _():
        s_cur = jnp.sum(q2 * twice(kcur_ref[...]), axis=1, keepdims=True)
        m_scr[...] = jnp.broadcast_to(s_cur, m_scr.shape)
        l_scr[...] = jnp.ones_like(l_scr)
        acc_scr[...] = twice(vcur_ref[...])

    ps = kc_refs[0].shape[0]
    rows = ps * nh
    rid = lax.broadcasted_iota(jnp.int32, (2 * nh, rows), 0)
    cid = lax.broadcasted_iota(jnp.int32, (2 * nh, rows), 1)
    own_head = (cid % nh) == (rid % nh)
    q2b = q2.astype(BF16)
    ss = [jnp.where(own_head, _dot_nt(q2b, kc[...].reshape(rows, LANES).astype(BF16)), -jnp.inf) for kc in kc_refs]
    m_old = m_scr[:, 0:1]
    m_new = m_old
    for s_j in ss:
        m_new = jnp.maximum(m_new, jnp.max(s_j, axis=1, keepdims=True))
    alpha = jnp.exp(m_old - m_new)
    ps_j = [jnp.exp(s_j - m_new) for s_j in ss]
    l_new = alpha * l_scr[:, 0:1]
    acc = alpha * acc_scr[...]
    for p_j, vc in zip(ps_j, vc_refs):
        l_new = l_new + jnp.sum(p_j, axis=1, keepdims=True)
        acc = acc + _dot(p_j.astype(BF16), vc[...].reshape(rows, LANES).astype(BF16))
    m_scr[...] = jnp.broadcast_to(m_new, m_scr.shape)
    l_scr[...] = jnp.broadcast_to(l_new, l_scr.shape)
    acc_scr[...] = acc

    @pl.when(p == nsteps - 1)
    def _():
        lam = _diff_lambda(lam_ref[...], lam_init)
        o = acc[:nh] * (1.0 / l_new[:nh]) - lam * (acc[nh:] * (1.0 / l_new[nh:]))
        o_ref[...] = _sub_ln(o, sub_ref[...], lam_init)


def _attn_sample(page_table, q3, cache_k, cache_v, kcur3, vcur3, lam_p, sub, l, lam_init, dh):
    bs, nh, _ = q3.shape
    npages = page_table.shape[1]
    ps = cache_k.shape[2]
    npp = _pick_tile(npages, (SAMPLE_PAGES_PER_STEP, 2, 1))
    nsteps = npages // npp
    cur = pl.BlockSpec((None, nh, LANES), lambda bi, p, pt: (bi, 0, 0))

    def page(j):
        return pl.BlockSpec((None, None, ps, nh, LANES), lambda bi, p, pt: (l, pt[bi, p * npp + j], 0, 0, 0))

    pages = [page(j) for j in range(npp)]
    grid_spec = pltpu.PrefetchScalarGridSpec(
        num_scalar_prefetch=1,
        grid=(bs, nsteps),
        in_specs=[cur] + pages + pages + [
            cur, cur,
            pl.BlockSpec((None,) + lam_p.shape[1:], lambda bi, p, pt: (l, 0, 0)),
            pl.BlockSpec((None, 1, LANES), lambda bi, p, pt: (l, 0, 0))],
        out_specs=cur,
        scratch_shapes=[pltpu.VMEM((2 * nh, LANES), F32)] * 3,
    )
    return pl.pallas_call(
        functools.partial(_attn_sample_body, lam_init=lam_init, dh=dh, nsteps=nsteps, npp=npp),
        out_shape=jax.ShapeDtypeStruct((bs, nh, LANES), F32),
        grid_spec=grid_spec,
        compiler_params=_cparams(("arbitrary", "arbitrary"), VMEM_SMALL),
        name="attn_sample",
    )(page_table, q3, *([cache_k] * npp), *([cache_v] * npp), kcur3, vcur3, lam_p, sub)


def _conv_body(x_ref, c0_ref, w_ref, b_ref, o_ref, carry, *, width):
    x = x_ref[...]
    tm = x.shape[0]

    @pl.when(pl.program_id(1) == 0)
    def _():
        carry[...] = c0_ref[...]

    hist = carry[...]
    carry[...] = x[tm - 8:tm, :]
    w = w_ref[...]
    row8 = lax.broadcasted_iota(jnp.int32, hist.shape, 0)
    acc = b_ref[...] + x * w[width - 1:width]
    for j in range(1, width):
        rolled = pltpu.roll(x, j, axis=0)
        head = jnp.where(row8 < j, pltpu.roll(hist, j, axis=0), rolled[0:8])
        xj = jnp.concatenate([head, rolled[8:]], axis=0)
        acc = acc + xj * w[width - 1 - j:width - j]
    o_ref[...] = _silu(acc)


def _conv(x3, conv0_8, w, bias, l):
    b, lseq, cd = x3.shape
    width = w.shape[1]
    tm = _pick_tile(lseq, (256, 128))
    nblk = lseq // tm
    return pl.pallas_call(
        functools.partial(_conv_body, width=width),
        out_shape=jax.ShapeDtypeStruct((b * lseq, cd), F32),
        grid=(b, nblk),
        in_specs=[pl.BlockSpec((None, tm, cd), lambda bi, i: (bi, i, 0)),
                  pl.BlockSpec((None, 8, cd), lambda bi, i: (bi, 0, 0)),
                  pl.BlockSpec((None, width, cd), lambda bi, i: (l, 0, 0)),
                  pl.BlockSpec((None, 1, cd), lambda bi, i: (l, 0, 0))],
        out_specs=pl.BlockSpec((tm, cd), lambda bi, i: (bi * nblk + i, 0)),
        scratch_shapes=[pltpu.VMEM((8, cd), F32)],
        compiler_params=_cparams(("arbitrary", "arbitrary"), VMEM_SMALL),
        name="mamba_conv",
    )(x3, conv0_8, w, bias)


def _ssd_body(xbc_ref, z_ref, dt_ref, par_ref, nrm_ref, h0_ref, o_ref, ho_ref, st_scr,
              *, t, valid, nchunks, n_groups, heads_per_group, p_dim, n_dim):
    c = pl.program_id(1)

    @pl.when(c == 0)
    def _():
        st_scr[...] = h0_ref[...]

    par = par_ref[...]
    dt = _softplus(dt_ref[...] + par[0:1])
    if valid < t * nchunks:
        ok = c * t + lax.broadcasted_iota(jnp.int32, dt.shape, 0) < valid
        dt = jnp.where(ok, dt, 0.0)
    a_neg = -jnp.exp(par[1:2])
    d_skip = par[2:3]
    ri = lax.broadcasted_iota(jnp.int32, (t, t), 0)
    ci = lax.broadcasted_iota(jnp.int32, (t, t), 1)
    tril = ci <= ri
    acs = _cumsum_rows(dt * a_neg)
    acs_t = acs.T
    tot = acs[t - 1:t, :]
    lane = lax.broadcasted_iota(jnp.int32, (t, LANES), 1)
    m0 = lane < p_dim
    m0r = lax.broadcasted_iota(jnp.int32, (1, LANES), 1) < p_dim
    prow = lax.broadcasted_iota(jnp.int32, (LANES, n_dim), 0) < p_dim
    n_heads = n_groups * heads_per_group
    xs_w = n_heads * p_dim
    pairs_per_group = heads_per_group // 2
    gw = heads_per_group * p_dim

    c_off = xs_w + n_groups * n_dim
    b_gs = [xbc_ref[:, xs_w + g * n_dim: xs_w + (g + 1) * n_dim].astype(BF16) for g in range(n_groups)]
    c_gs = [xbc_ref[:, c_off + g * n_dim: c_off + (g + 1) * n_dim].astype(BF16) for g in range(n_groups)]
    cbs = [_dot_nt(c_g, b_g) for c_g, b_g in zip(c_gs, b_gs)]
    pairs = [(g, g * pairs_per_group + pp) for g in range(n_groups) for pp in range(pairs_per_group)]

    def prep(g, pr):
        h_lo, h_hi = 2 * pr, 2 * pr + 1
        xs = xbc_ref[:, pr * LANES:(pr + 1) * LANES]
        col_lo, col_hi = acs[:, h_lo:h_lo + 1], acs[:, h_hi:h_hi + 1]
        l_lo = jnp.exp(jnp.where(tril, col_lo - acs_t[h_lo:h_lo + 1, :], -1e30))
        l_hi = jnp.exp(jnp.where(tril, col_hi - acs_t[h_hi:h_hi + 1, :], -1e30))
        x_dt = xs * jnp.where(m0, dt[:, h_lo:h_lo + 1], dt[:, h_hi:h_hi + 1])
        colp = jnp.where(m0, col_lo, col_hi)
        totp = jnp.where(m0r, tot[:, h_lo:h_lo + 1], tot[:, h_hi:h_hi + 1])
        chunk_decay = jnp.where(prow, jnp.exp(tot[:, h_lo:h_lo + 1]), jnp.exp(tot[:, h_hi:h_hi + 1]))
        dp = jnp.where(m0r, d_skip[:, h_lo:h_lo + 1], d_skip[:, h_hi:h_hi + 1])
        return ((cbs[g] * l_lo).astype(BF16), (cbs[g] * l_hi).astype(BF16), x_dt.astype(BF16), jnp.exp(colp),
                (x_dt * jnp.exp(totp - colp)).T.astype(BF16), chunk_decay, dp * xs)

    preps = [prep(g, pr) for g, pr in pairs]
    sts = [st_scr[pr] for _, pr in pairs]
    y_diags = [jnp.where(m0, _dot(q[0], q[2]), _dot(q[1], q[2])) for q in preps]
    y_offs = [q[3] * _dot_nt(c_gs[g], st.astype(BF16)) for (g, _), q, st in zip(pairs, preps, sts)]
    contribs = [_dot(q[4], b_gs[g]) for (g, _), q in zip(pairs, preps)]
    for (_, pr), q, st, contrib in zip(pairs, preps, sts, contribs):
        st_scr[pr] = q[5] * st + contrib
    ys = [yd + yo + q[6] for yd, yo, q in zip(y_diags, y_offs, preps)]
    for g in range(n_groups):
        yg_parts = ys[g * pairs_per_group:(g + 1) * pairs_per_group]
        yg = yg_parts[0] if len(yg_parts) == 1 else jnp.concatenate(yg_parts, axis=1)
        yz = yg * _silu(z_ref[:, g * gw:(g + 1) * gw])
        ms = jnp.mean(yz * yz, axis=-1, keepdims=True)
        o_ref[:, g * gw:(g + 1) * gw] = (yz * lax.rsqrt(ms + MB_NORM_EPS) * nrm_ref[:, g * gw:(g + 1) * gw]).astype(BF16)

    @pl.when(c == nchunks - 1)
    def _():
        ho_ref[...] = st_scr[...]


def _ssd(xbc, uz, udt, par, nrm, h0, l, b, lseq, valid, n_groups, heads_per_group, p_dim, n_dim):
    cd = xbc.shape[1]
    cc = uz.shape[1]
    t = SSD_CHUNK_T
    nchunks = lseq // t
    npair = h0.shape[1]
    return pl.pallas_call(
        functools.partial(_ssd_body, t=t, valid=valid, nchunks=nchunks, n_groups=n_groups,
                          heads_per_group=heads_per_group, p_dim=p_dim, n_dim=n_dim),
        out_shape=(jax.ShapeDtypeStruct((b * lseq, cc), BF16),
                   jax.ShapeDtypeStruct(h0.shape, F32)),
        grid=(b, nchunks),
        in_specs=[pl.BlockSpec((t, cd), lambda bi, ci: (bi * nchunks + ci, 0)),
                  pl.BlockSpec((t, cc), lambda bi, ci: (bi * nchunks + ci, 0)),
                  pl.BlockSpec((t, LANES), lambda bi, ci: (bi * nchunks + ci, 0)),
                  pl.BlockSpec((None, 8, LANES), lambda bi, ci: (l, 0, 0)),
                  pl.BlockSpec((None, 1, cc), lambda bi, ci: (l, 0, 0)),
                  pl.BlockSpec((None, npair, LANES, n_dim), lambda bi, ci: (bi, 0, 0, 0))],
        out_specs=(pl.BlockSpec((t, cc), lambda bi, ci: (bi * nchunks + ci, 0)),
                   pl.BlockSpec((None, npair, LANES, n_dim), lambda bi, ci: (bi, 0, 0, 0))),
        scratch_shapes=[pltpu.VMEM((npair, LANES, n_dim), F32)],
        compiler_params=_cparams(("arbitrary", "arbitrary"), VMEM_SMALL),
        name="ssd",
    )(xbc, uz, udt, par, nrm, h0)


def _pad_last(x, n):
    return jnp.pad(x, [(0, 0)] * (x.ndim - 1) + [(0, n - x.shape[-1])])


def _pad_rows(x, n):
    return jnp.pad(x, [(0, n - x.shape[0])] + [(0, 0)] * (x.ndim - 1))


def _wkv_to_blockdiag(wkv):
    b, h, hv, hk = wkv.shape
    w = wkv.reshape(b, h // 2, 2, hv, hk)
    z = jnp.zeros_like(w[:, :, 0])
    top = jnp.concatenate([w[:, :, 0], z], axis=-1)
    bot = jnp.concatenate([z, w[:, :, 1]], axis=-1)
    return jnp.concatenate([top, bot], axis=-2)


def _blockdiag_to_wkv(s, hv, hk):
    b, p = s.shape[:2]
    return jnp.stack([s[:, :, :hv, :hk], s[:, :, hv:, hk:]], axis=2).reshape(b, 2 * p, hv, hk)


def _first_token_rows(x_rows, bs, t):
    return jnp.pad(x_rows[:bs, None, :], ((0, 0), (0, t - 1), (0, 0)))


def kernel(x_prompt, x_sample, cache_k, cache_v, page_table, state_rwkv_shift, state_rwkv_wkv, state_conv, state_ssm, g_ffn1, w_ffn1_gate, w_ffn1_up, w_ffn1_down, g_mix, w_in, rw_mu, rw_w0, rw_w_up, rw_a0, rw_a_up, rw_g_up, rw_k_k, rw_k_a, rw_r_k, rw_ln_w, rw_ln_b, da_q_norm, da_k_norm, da_lambda, da_subln, mb_conv_w, mb_conv_b, mb_dt_bias, mb_A_log, mb_D, mb_norm, w_br, w_o, g_ffn2, w_ffn2_gate, w_ffn2_up, w_ffn2_down):
    bp, lp, d = x_prompt.shape
    bs, ls, _ = x_sample.shape
    depth = g_ffn1.shape[0]
    na = state_rwkv_shift.shape[-1]
    h_a, hd_a = state_rwkv_wkv.shape[2], state_rwkv_wkv.shape[3]
    c_a = h_a * hd_a
    lw, la, lg = rw_w_up.shape[1], rw_a_up.shape[1], rw_g_up.shape[1]
    h_b, cqk_h = cache_k.shape[3], cache_k.shape[4]
    vd_b = cache_v.shape[4]
    dh = cqk_h // 2
    c_qk, c_b = h_b * cqk_h, h_b * vd_b
    h_c, p_c, n_c = state_ssm.shape[2], state_ssm.shape[3], state_ssm.shape[4]
    c_c = h_c * p_c
    conv_w, conv_dim = mb_conv_w.shape[1], mb_conv_w.shape[2]
    g_c = (conv_dim - c_c) // (2 * n_c)
    e_c = h_c // g_c
    past = page_table.shape[1] * cache_k.shape[2]
    assert ls == 1, "sample group: one new token per sequence"
    assert hd_a * 2 == LANES and cqk_h == LANES and vd_b == LANES and p_c * 2 == LANES and n_c == LANES
    assert e_c % 2 == 0 and h_c <= LANES and conv_w <= 8
    assert na == 3 * c_a + lw + la + lg

    bf = lambda w: w.astype(BF16)
    nap = _round_up(na, LANES)
    seg_sizes = [na, c_qk, c_qk, c_b, c_c, conv_dim, h_c, 3 * d]
    offs = [0]
    for s in seg_sizes:
        offs.append(offs[-1] + s)
    assert offs[-1] == w_in.shape[-1]
    w_t = jnp.swapaxes(w_in, 1, 2)
    win = [(offs[i], _round_up(seg_sizes[i], LANES)) for i in range(len(seg_sizes))]
    assert win[-1][0] + win[-1][1] <= w_t.shape[1]
    w_br3 = bf(w_br).reshape(depth, 3, c_a, d)
    w_o_b = bf(w_o)
    tail_w = nap - 3 * c_a
    w_lora = jnp.zeros((depth, tail_w, 3 * c_a), BF16)
    w_lora = w_lora.at[:, :lw, :c_a].set(bf(rw_w_up))
    w_lora = w_lora.at[:, lw:lw + la, c_a:2 * c_a].set(bf(rw_a_up))
    w_lora = w_lora.at[:, lw + la:lw + la + lg, 2 * c_a:].set(bf(rw_g_up))

    row3 = lambda x: x.reshape(depth, 1, -1)
    g1, gm, g2 = row3(g_ffn1), row3(g_mix), row3(g_ffn2)
    mu = row3(_pad_last(rw_mu, nap))
    rw_par = jnp.stack([rw_w0, rw_a0, rw_k_k, rw_k_a, rw_r_k.reshape(depth, c_a), rw_ln_w, rw_ln_b,
                        jnp.zeros_like(rw_w0)], axis=1)
    qn = row3(da_q_norm)
    kn = row3(da_k_norm)
    sub = row3(da_subln)
    mb_par = jnp.stack([_pad_last(mb_dt_bias, LANES), _pad_last(mb_A_log, LANES), _pad_last(mb_D, LANES)]
                       + [jnp.zeros((depth, LANES), F32)] * 5, axis=1)
    mb_nrm = row3(mb_norm)
    conv_b = row3(mb_conv_b)

    half = dh // 2
    inv = ROPE_THETA ** (-jnp.arange(half, dtype=F32) / half)

    def rope_tables(pos):
        ang = pos.astype(F32)[:, None] * inv[None, :]
        cos, sin = jnp.cos(ang), jnp.sin(ang)
        return jnp.tile(cos, (1, LANES // half)), jnp.tile(jnp.concatenate([-sin, sin], axis=1), (1, LANES // dh))

    cos_p, sin_p = rope_tables(jnp.arange(lp))
    ms_rows = _round_up(bs * ls, BF16_ROWS)
    cos_s, sin_s = rope_tables(jnp.full((ms_rows,), past, jnp.int32))

    xp = x_prompt.reshape(bp * lp, d)
    xs = _pad_rows(x_sample.reshape(bs * ls, d), ms_rows)
    t_a, t_c = RWKV_CHUNK, SSD_CHUNK_T
    outs_p = [[] for _ in range(6)]
    outs_s = [[] for _ in range(6)]

    def project(x, l):
        h = _rmsnorm(x, gm, l)
        ua = _proj(h, w_t, l, *win[0])
        uq = _proj(h, w_t, l, *win[1])
        uk = _proj(h, w_t, l, *win[2])
        uv, uv_b = _proj(h, w_t, l, *win[3], bf16_copy=True)
        uz = _proj(h, w_t, l, *win[4])
        uxbc = _proj(h, w_t, l, *win[5])
        udt = _proj(h, w_t, l, *win[6])
        ug = _proj(h, w_t, l, *win[7])
        return ua, uq, uk, uv, uv_b, uz, uxbc, udt, ug

    def mix_out(x, o_a, o_b, o_c, ug, l):
        merged = _merge(o_a, o_b, o_c, w_br3, ug, l)
        return _mm(merged, w_o_b, l, residual=x)

    for l in range(depth):
        lam_init = 0.8 - 0.6 * math.exp(-0.3 * l)

        xs, wg_b, wu_b, wd_b = _ffn_cast(xs, g1, l, w_ffn1_gate, w_ffn1_up, w_ffn1_down)

        xp = _ffn(xp, g1, l, wg_b, wu_b, wd_b, 0)
        ua, uq, uk, uv, uv_b, uz, uxbc, udt, ug = project(xp, l)
        ua3 = ua.reshape(bp, lp, nap)
        o_a, s_out = _rwkv_scan(ua3, jnp.zeros((bp, 1, nap), F32), mu, w_lora, rw_par,
                                jnp.zeros((bp, c_a // LANES, LANES, LANES), F32), l, lp, 3 * c_a, lw, la)
        q_r, k_r, k_rb = _qk_prep(uq, uk, cos_p, sin_p, qn, kn, l, dh)
        o_b = _attn_prompt(q_r, k_rb, uv_b, da_lambda, sub, l, bp, lp, lam_init, dh)
        xbc_act = _conv(uxbc.reshape(bp, lp, conv_dim), jnp.zeros((bp, 8, conv_dim), F32), mb_conv_w, conv_b, l)
        o_c, h_out = _ssd(xbc_act, uz, udt, mb_par, mb_nrm, jnp.zeros((bp, h_c // 2, LANES, n_c), F32),
                          l, bp, lp, lp, g_c, e_c, p_c, n_c)
        xp = mix_out(xp, o_a, o_b, o_c, ug, l)
        outs_p[0].append(k_r.reshape(bp, lp, h_b, cqk_h))
        outs_p[1].append(uv.reshape(bp, lp, h_b, vd_b))
        outs_p[2].append(ua3[:, lp - 1, :na])
        outs_p[3].append(_blockdiag_to_wkv(s_out, hd_a, hd_a))
        outs_p[4].append(uxbc.reshape(bp, lp, conv_dim)[:, lp - (conv_w - 1):])
        outs_p[5].append(h_out.reshape(bp, h_c, p_c, n_c))

        ua, uq, uk, uv, uv_b, uz, uxbc, udt, ug = project(xs, l)
        ua3 = _first_token_rows(ua, bs, t_a)
        shift0 = _pad_last(state_rwkv_shift[l], nap)[:, None, :]
        o_a_seq, s_out = _rwkv_scan(ua3, shift0, mu, w_lora, rw_par, _wkv_to_blockdiag(state_rwkv_wkv[l]),
                                    l, ls, 3 * c_a, lw, la)
        o_a = _pad_rows(o_a_seq.reshape(bs, t_a, c_a)[:, 0], ms_rows)
        q_r, k_r, _ = _qk_prep(uq, uk, cos_s, sin_s, qn, kn, l, dh)
        o_b3 = _attn_sample(page_table, q_r[:bs].reshape(bs, h_b, cqk_h), cache_k, cache_v,
                            k_r[:bs].reshape(bs, h_b, cqk_h), uv[:bs].reshape(bs, h_b, vd_b),
                            da_lambda, sub, l, lam_init, dh)
        o_b = _pad_rows(o_b3.reshape(bs, c_b), ms_rows).astype(BF16)
        conv0 = state_conv[l]
        conv0_8 = jnp.pad(conv0, ((0, 0), (8 - (conv_w - 1), 0), (0, 0)))
        xbc_act = _conv(_first_token_rows(uxbc, bs, t_c), conv0_8, mb_conv_w, conv_b, l)
        h0 = state_ssm[l].reshape(bs, h_c // 2, LANES, n_c)
        o_c_seq, h_out = _ssd(xbc_act, _first_token_rows(uz, bs, t_c).reshape(bs * t_c, c_c),
                              _first_token_rows(udt, bs, t_c).reshape(bs * t_c, LANES),
                              mb_par, mb_nrm, h0, l, bs, t_c, ls, g_c, e_c, p_c, n_c)
        o_c = _pad_rows(o_c_seq.reshape(bs, t_c, c_c)[:, 0], ms_rows)
        xs = mix_out(xs, o_a, o_b, o_c, ug, l)
        xs, wg_b, wu_b, wd_b = _ffn_cast(xs, g2, l, w_ffn2_gate, w_ffn2_up, w_ffn2_down)
        xp = _ffn(xp, g2, l, wg_b, wu_b, wd_b, 0)
        outs_s[0].append(k_r[:bs].reshape(bs, ls, h_b, cqk_h))
        outs_s[1].append(uv[:bs].reshape(bs, ls, h_b, vd_b))
        outs_s[2].append(ua[:bs, :na])
        outs_s[3].append(_blockdiag_to_wkv(s_out, hd_a, hd_a))
        outs_s[4].append(jnp.concatenate([conv0, uxbc[:bs, None, :]], axis=1)[:, ls:])
        outs_s[5].append(h_out.reshape(bs, h_c, p_c, n_c))

    yp = xp.reshape(bp, lp, d)
    ys = xs[:bs * ls].reshape(bs, ls, d)
    return (yp, ys) + tuple(jnp.stack(o) for o in outs_p) + tuple(jnp.stack(o) for o in outs_s)
```
